```python
import jax, jax.numpy as jnp
from jax import lax
import numpy as np

D_MODEL = 1024
BATCH = 4
SEQ = 8192
DEPTH = 1
DEC_BATCH = 8
DEC_SEQ = 8192
PAST_LEN = 128

N_HEADS_A = 8
HEAD_DIM_A = 64
DILATED_CONFIGS = ((128, 1), (512, 4), (2048, 16))
N_HEADS_B = 8
Q_LORA_RANK = 256
KV_LORA_RANK = 128
QK_NOPE_DIM = 64
QK_ROPE_DIM = 32
V_HEAD_DIM = 64

WIDTH_A = N_HEADS_A * HEAD_DIM_A
WIDTH_B = N_HEADS_B * V_HEAD_DIM
MIX_WIDTH = WIDTH_A + WIDTH_B
IN_PROJ_WIDTH = 3 * WIDTH_A + Q_LORA_RANK + KV_LORA_RANK + QK_ROPE_DIM
D_FF = 2816
CONV_WIDTH = 3

ROPE_THETA = 10000.0
EPS = 1e-6
Q_BLOCK = 128
NEG = -1e30

kernel_name = "hymba_dilated_mla_convffn_encoder"


def rmsnorm(x, g):
    xf = x.astype(jnp.float32)
    y = xf * lax.rsqrt(jnp.mean(xf * xf, axis=-1, keepdims=True) + EPS)
    return (y * g.astype(jnp.float32)).astype(x.dtype)


def rope(t):
    s, dim = t.shape[1], t.shape[-1]
    half = dim // 2
    inv_freq = ROPE_THETA ** (-jnp.arange(half, dtype=jnp.float32) * (2.0 / dim))
    ang = jnp.arange(s, dtype=jnp.float32)[:, None] * inv_freq[None, :]
    cos = jnp.cos(ang)[None, :, None, :]
    sin = jnp.sin(ang)[None, :, None, :]
    tf = t.astype(jnp.float32)
    t1, t2 = tf[..., :half], tf[..., half:]
    return jnp.concatenate([t1 * cos - t2 * sin, t2 * cos + t1 * sin], axis=-1).astype(t.dtype)


def window_attention(q, k, v, radius):
    n, l, h, hd = q.shape
    blk = radius
    nb = -(-l // blk)
    lp = nb * blk
    qb = jnp.pad(q, ((0, 0), (0, lp - l), (0, 0), (0, 0))).reshape(n, nb, blk, h, hd)
    pad_kv = ((0, 0), (blk, lp - l + blk), (0, 0), (0, 0))
    kp = jnp.pad(k, pad_kv).reshape(n, nb + 2, blk, h, hd)
    vp = jnp.pad(v, pad_kv).reshape(n, nb + 2, blk, h, hd)
    kb = jnp.concatenate([kp[:, :-2], kp[:, 1:-1], kp[:, 2:]], axis=2)
    vb = jnp.concatenate([vp[:, :-2], vp[:, 1:-1], vp[:, 2:]], axis=2)
    sc = jnp.einsum('nbqhd,nbkhd->nbhqk', qb, kb, preferred_element_type=jnp.float32) * (hd ** -0.5)
    qpos = jnp.arange(nb)[:, None] * blk + jnp.arange(blk)[None, :]
    kpos = jnp.arange(nb)[:, None] * blk - blk + jnp.arange(3 * blk)[None, :]
    rel = kpos[:, None, :] - qpos[:, :, None]
    valid = (jnp.abs(rel) <= radius) & (kpos[:, None, :] >= 0) & (kpos[:, None, :] < l)
    sc = jnp.where(valid[None, :, None], sc, NEG)
    m = jnp.max(sc, axis=-1, keepdims=True)
    p = jnp.exp(sc - m)
    den = jnp.sum(p, axis=-1)
    o = jnp.einsum('nbhqk,nbkhd->nbqhd', p, vb.astype(jnp.float32))
    o = o / jnp.transpose(den, (0, 1, 3, 2))[..., None]
    lse = jnp.transpose(m[..., 0] + jnp.log(den), (0, 1, 3, 2))
    return o.reshape(n, lp, h, hd)[:, :l], lse.reshape(n, lp, h)[:, :l]


def dilated_attention(q, k, v):
    b, s, h, hd = q.shape
    outs, lses = [], []
    for window, dil in DILATED_CONFIGS:
        radius = window // 2 // dil
        ls = s // dil

        def to_res(t):
            return t.reshape(b, ls, dil, h, hd).transpose(0, 2, 1, 3, 4).reshape(b * dil, ls, h, hd)

        o, lse = window_attention(to_res(q), to_res(k), to_res(v), radius)
        outs.append(o.reshape(b, dil, ls, h, hd).transpose(0, 2, 1, 3, 4).reshape(b, s, h, hd))
        lses.append(lse.reshape(b, dil, ls, h).transpose(0, 2, 1, 3).reshape(b, s, h))
    wts = jax.nn.softmax(jnp.stack(lses, axis=0), axis=0)
    out = jnp.sum(wts[..., None] * jnp.stack(outs, axis=0), axis=0)
    return out.astype(q.dtype)


def dense_attention(q, k, v):
    b, s, h, dk = q.shape
    dv = v.shape[-1]
    nb = s // Q_BLOCK
    qb = q.reshape(b, nb, Q_BLOCK, h, dk).transpose(1, 0, 2, 3, 4)
    scale = dk ** -0.5

    def one_block(qblk):
        sc = jnp.einsum('bqhd,bkhd->bhqk', qblk, k, preferred_element_type=jnp.float32) * scale
        p = jax.nn.softmax(sc, axis=-1)
        return jnp.einsum('bhqk,bkhd->bqhd', p.astype(v.dtype), v, preferred_element_type=jnp.float32)

    o = lax.map(one_block, qb)
    return o.transpose(1, 0, 2, 3, 4).reshape(b, s, h, dv).astype(q.dtype)


def depthwise_conv(u, w, bias):
    c = u.shape[-1]
    y = lax.conv_general_dilated(u, w[:, None, :], window_strides=(1,),
                                 padding=((CONV_WIDTH // 2, CONV_WIDTH // 2),),
                                 dimension_numbers=('NWC', 'WIO', 'NWC'),
                                 feature_group_count=c)
    return y + bias


def encoder_layer(x, g_attn, w_in, g_q_lora, w_uq, g_kv_lora, w_ukv, g_out_a, g_out_b, w_o,
                  g_ffn, w_up, w_conv, b_conv, w_down):
    b, s, _ = x.shape
    hn = rmsnorm(x, g_attn)
    proj = hn @ w_in
    splits = np.cumsum([WIDTH_A, WIDTH_A, WIDTH_A, Q_LORA_RANK, KV_LORA_RANK]).tolist()
    qa, ka, va, c_q, c_kv, k_rope = jnp.split(proj, splits, axis=-1)
    qa = rope(qa.reshape(b, s, N_HEADS_A, HEAD_DIM_A))
    ka = rope(ka.reshape(b, s, N_HEADS_A, HEAD_DIM_A))
    va = va.reshape(b, s, N_HEADS_A, HEAD_DIM_A)
    o_a = dilated_attention(qa, ka, va).reshape(b, s, WIDTH_A)
    qb = (rmsnorm(c_q, g_q_lora) @ w_uq).reshape(b, s, N_HEADS_B, QK_NOPE_DIM + QK_ROPE_DIM)
    q_nope, q_rot = qb[..., :QK_NOPE_DIM], rope(qb[..., QK_NOPE_DIM:])
    kv = (rmsnorm(c_kv, g_kv_lora) @ w_ukv).reshape(b, s, N_HEADS_B, QK_NOPE_DIM + V_HEAD_DIM)
    k_nope, vb = kv[..., :QK_NOPE_DIM], kv[..., QK_NOPE_DIM:]
    k_rot = jnp.broadcast_to(rope(k_rope[:, :, None, :]), (b, s, N_HEADS_B, QK_ROPE_DIM))
    o_b = dense_attention(jnp.concatenate([q_nope, q_rot], axis=-1),
                          jnp.concatenate([k_nope, k_rot], axis=-1), vb).reshape(b, s, WIDTH_B)
    mix = jnp.concatenate([rmsnorm(o_a, g_out_a), rmsnorm(o_b, g_out_b)], axis=-1) @ w_o
    x = x + mix
    u = depthwise_conv(rmsnorm(x, g_ffn) @ w_up, w_conv, b_conv)
    gate, val = u[..., :D_FF], u[..., D_FF:]
    x = x + (jax.nn.silu(gate) * val) @ w_down
    return x


def setup_inputs(seed: int = 0) -> dict:
    key = jax.random.key(seed)
    ks = jax.random.split(key, 20)
    f32 = jnp.float32

    def w(k, shape, fan_in):
        return jax.random.normal(k, shape, f32) * (fan_in ** -0.5)

    def gain(k, shape):
        return 1.0 + 0.1 * jax.random.normal(k, shape, f32)

    return {
        "x_prompt": jax.random.normal(ks[0], (BATCH, SEQ, D_MODEL), f32),
        "x_sample": jax.random.normal(ks[1], (DEC_BATCH, DEC_SEQ, D_MODEL), f32),
        "g_attn": gain(ks[2], (DEPTH, D_MODEL)),
        "w_in": w(ks[3], (DEPTH, D_MODEL, IN_PROJ_WIDTH), D_MODEL),
        "g_q_lora": gain(ks[4], (DEPTH, Q_LORA_RANK)),
        "w_uq": w(ks[5], (DEPTH, Q_LORA_RANK, N_HEADS_B * (QK_NOPE_DIM + QK_ROPE_DIM)), Q_LORA_RANK),
        "g_kv_lora": gain(ks[6], (DEPTH, KV_LORA_RANK)),
        "w_ukv": w(ks[7], (DEPTH, KV_LORA_RANK, N_HEADS_B * (QK_NOPE_DIM + V_HEAD_DIM)), KV_LORA_RANK),
        "g_out_a": gain(ks[8], (DEPTH, WIDTH_A)),
        "g_out_b": gain(ks[9], (DEPTH, WIDTH_B)),
        "w_o": w(ks[10], (DEPTH, MIX_WIDTH, D_MODEL), MIX_WIDTH),
        "g_ffn": gain(ks[11], (DEPTH, D_MODEL)),
        "w_up": w(ks[12], (DEPTH, D_MODEL, 2 * D_FF), D_MODEL),
        "w_conv": w(ks[13], (DEPTH, CONV_WIDTH, 2 * D_FF), CONV_WIDTH),
        "b_conv": 0.02 * jax.random.normal(ks[14], (DEPTH, 2 * D_FF), f32),
        "w_down": w(ks[15], (DEPTH, D_FF, D_MODEL), D_FF),
        "g_final": gain(ks[16], (D_MODEL,)),
    }


def reference(x_prompt, x_sample, g_attn, w_in, g_q_lora, w_uq, g_kv_lora, w_ukv, g_out_a, g_out_b,
              w_o, g_ffn, w_up, w_conv, b_conv, w_down, g_final):
    def trunk(x):
        for layer in range(DEPTH):
            x = encoder_layer(x, g_attn[layer], w_in[layer], g_q_lora[layer], w_uq[layer],
                              g_kv_lora[layer], w_ukv[layer], g_out_a[layer], g_out_b[layer],
                              w_o[layer], g_ffn[layer], w_up[layer], w_conv[layer], b_conv[layer],
                              w_down[layer])
        return rmsnorm(x, g_final)

    y_prompt = trunk(x_prompt)
    y_sample = trunk(x_sample)
    return (y_prompt, y_sample)
```

```python
import functools

import jax
import jax.numpy as jnp
from jax import lax
from jax.experimental import pallas as pl
from jax.experimental.pallas import tpu as pltpu

D_MODEL = 1024
N_HEADS_A = 8
HEAD_DIM_A = 64
DILATED_CONFIGS = ((128, 1), (512, 4), (2048, 16))
N_HEADS_B = 8
Q_LORA_RANK = 256
KV_LORA_RANK = 128
QK_NOPE_DIM = 64
QK_ROPE_DIM = 32
V_HEAD_DIM = 64
WIDTH_A = N_HEADS_A * HEAD_DIM_A
WIDTH_B = N_HEADS_B * V_HEAD_DIM
D_FF = 2816
ROPE_THETA = 10000.0
EPS = 1e-6
NEG = -1e30

LANES = 128
SUBLANES = 8
HEAD_PAD = LANES
WIDTH_B_PAD = N_HEADS_B * HEAD_PAD
VMEM_LIMIT_BYTES = 56 * 1024 * 1024

PROJ_ROWS = 512
A_ROWS = 128
A_RADIUS = 64
B_Q_ROWS = 512
B_K_ROWS = 512
FFN_COLS = 256

_OFF_Q, _OFF_K, _OFF_QR, _OFF_KR, _OFF_V = 0, 512, 1024, 1536, 2048
_OFF_CQ = 2560
_OFF_CKV = _OFF_CQ + Q_LORA_RANK
_EXT_WIDTH = _OFF_CKV + 3 * LANES

_BF16 = jnp.bfloat16
_F32 = jnp.float32


def _rms(xf, g):
    return xf * lax.rsqrt(jnp.mean(xf * xf, axis=-1, keepdims=True) + EPS) * g


def _dot(a, b):
    return jnp.dot(a, b, preferred_element_type=_F32)


def _dot_nt(a, b):
    return lax.dot_general(a, b, (((1,), (1,)), ((), ())), preferred_element_type=_F32)


def _proj_kernel(x_ref, tab_a_ref, tab_b_ref, g_attn_ref, w_ext_ref, g_q_ref, w_uq_ref, g_kv_ref, w_ukv_ref,
                 qa_ref, ka_ref, va_ref, qb_ref, kb_ref, vb_ref):
    hn = _rms(x_ref[...], g_attn_ref[...]).astype(_BF16)

    def proj(lo, width):
        return _dot(hn, w_ext_ref[:, lo:lo + width])

    def tab(ref, j):
        return ref[:, j * LANES:(j + 1) * LANES]

    for off_main, off_rot, j, out_ref in ((_OFF_Q, _OFF_QR, 0, qa_ref), (_OFF_K, _OFF_KR, 2, ka_ref)):
        main, rot = proj(off_main, WIDTH_A), proj(off_rot, WIDTH_A)
        cos, sin = tab(tab_a_ref, j), tab(tab_a_ref, j + 1)
        for g in range(WIDTH_A // LANES):
            sl = slice(g * LANES, (g + 1) * LANES)
            out_ref[:, sl] = (main[:, sl] * cos + rot[:, sl] * sin).astype(_BF16)
    va_ref[...] = proj(_OFF_V, WIDTH_A).astype(_BF16)

    cqn = _rms(proj(_OFF_CQ, Q_LORA_RANK), g_q_ref[...]).astype(_BF16)
    qb = _dot(cqn, w_uq_ref[...])
    cos, sin = tab(tab_b_ref, 0), tab(tab_b_ref, 1)
    for h in range(N_HEADS_B):
        sl = slice(h * HEAD_PAD, (h + 1) * HEAD_PAD)
        rot = qb[:, WIDTH_B_PAD + h * HEAD_PAD:WIDTH_B_PAD + (h + 1) * HEAD_PAD]
        qb_ref[:, sl] = (qb[:, sl] * cos + rot * sin).astype(_BF16)

    rest = proj(_OFF_CKV, 3 * LANES)
    ckvn = _rms(rest[:, :KV_LORA_RANK], g_kv_ref[...]).astype(_BF16)
    kv = _dot(ckvn, w_ukv_ref[...])
    k_rot = rest[:, LANES:2 * LANES] * tab(tab_b_ref, 2) + rest[:, 2 * LANES:] * tab(tab_b_ref, 3)
    lane = lax.broadcasted_iota(jnp.int32, (1, HEAD_PAD), 1)
    one_col = (lane == V_HEAD_DIM).astype(_F32)
    for h in range(N_HEADS_B):
        sl = slice(h * HEAD_PAD, (h + 1) * HEAD_PAD)
        kb_ref[:, sl] = (kv[:, sl] + k_rot).astype(_BF16)
        vb_ref[:, sl] = (kv[:, WIDTH_B_PAD + h * HEAD_PAD:WIDTH_B_PAD + (h + 1) * HEAD_PAD] + one_col).astype(_BF16)


def _const_spec(shape):
    return pl.BlockSpec(shape, lambda *_: (0,) * len(shape))


def _proj_call(x, tab_a, tab_b, g_attn, w_ext, g_q, w_uq, g_kv, w_ukv):
    b, s, _ = x.shape
    rows = PROJ_ROWS
    tok = lambda width: pl.BlockSpec((None, rows, width), lambda bi, i: (bi, i, 0))
    tabs = lambda width: pl.BlockSpec((rows, width), lambda bi, i: (i, 0))
    out_a = jax.ShapeDtypeStruct((b, s, WIDTH_A), _BF16)
    out_b = jax.ShapeDtypeStruct((b, s, WIDTH_B_PAD), _BF16)
    return pl.pallas_call(
        _proj_kernel,
        grid=(b, s // rows),
        in_specs=[tok(D_MODEL), tabs(4 * LANES), tabs(4 * LANES),
                  _const_spec(g_attn.shape), _const_spec(w_ext.shape), _const_spec(g_q.shape),
                  _const_spec(w_uq.shape), _const_spec(g_kv.shape), _const_spec(w_ukv.shape)],
        out_specs=[tok(WIDTH_A)] * 3 + [tok(WIDTH_B_PAD)] * 3,
        out_shape=[out_a] * 3 + [out_b] * 3,
        name="proj",
        compiler_params=pltpu.CompilerParams(dimension_semantics=("parallel", "parallel"),
                                             vmem_limit_bytes=VMEM_LIMIT_BYTES),
    )(x, tab_a, tab_b, g_attn, w_ext, g_q, w_uq, g_kv, w_ukv)


def _mixer_a_kernel(q_ref, k_ref, v_ref, o_ref, lse_ref):
    length = q_ref.shape[0]
    window = A_ROWS + 2 * A_RADIUS
    lane = lax.broadcasted_iota(jnp.int32, (1, LANES), 1)
    first_head = lane < HEAD_DIM_A
    col_minus_row = (lax.broadcasted_iota(jnp.int32, (A_ROWS, window), 1)
                     - lax.broadcasted_iota(jnp.int32, (A_ROWS, window), 0))

    def tile(t, carry):
        a0 = pl.multiple_of(t * A_ROWS, A_ROWS)
        ws = pl.multiple_of(jnp.clip(a0 - A_RADIUS, 0, length - window), A_RADIUS)
        valid = jnp.abs(col_minus_row + (ws - a0)) <= A_RADIUS
        q = q_ref[pl.ds(a0, A_ROWS), :]
        kw = k_ref[pl.ds(ws, window), :]
        vw = v_ref[pl.ds(ws, window), :]
        outs, lses = [], []
        for head_sel in (first_head, jnp.logical_not(first_head)):
            s = _dot_nt(jnp.where(head_sel, q, jnp.zeros_like(q)), kw)
            s = jnp.where(valid, s, NEG)
            m = jnp.max(s, axis=-1, keepdims=True)
            p = jnp.exp(s - m)
            den = jnp.sum(p, axis=-1, keepdims=True)
            outs.append(_dot(p.astype(_BF16), vw) / den)
            lses.append(m + jnp.log(den))
        o_ref[pl.ds(a0, A_ROWS), :] = jnp.where(first_head, outs[0], outs[1]).astype(_BF16)
        lse_ref[pl.ds(a0, A_ROWS), :] = jnp.where(first_head, lses[0], lses[1])
        return carry

    lax.fori_loop(0, length // A_ROWS, tile, 0)


def _mixer_a_call(qa, ka, va, dil):
    b, s, _ = qa.shape
    length = s // dil
    assert length % A_ROWS == 0 and length >= A_ROWS + 2 * A_RADIUS
    groups = WIDTH_A // LANES
    view = lambda t: t.reshape(b, length, dil * WIDTH_A)
    spec = pl.BlockSpec((None, length, LANES), lambda bi, r, g: (bi, 0, r * groups + g))
    o, lse = pl.pallas_call(
        _mixer_a_kernel,
        grid=(b, dil, groups),
        in_specs=[spec] * 3,
        out_specs=[spec] * 2,
        out_shape=[jax.ShapeDtypeStruct((b, length, dil * WIDTH_A), _BF16),
                   jax.ShapeDtypeStruct((b, length, dil * WIDTH_A), _F32)],
        name=f"mixer_a_dil{dil}",
        compiler_params=pltpu.CompilerParams(dimension_semantics=("parallel",) * 3,
                                             vmem_limit_bytes=VMEM_LIMIT_BYTES),
    )(view(qa), view(ka), view(va))
    return o.reshape(b, s, WIDTH_A), lse.reshape(b, s, WIDTH_A)


def _mixer_b_kernel(q_ref, k_ref, v_ref, o_ref):
    rows = q_ref.shape[0]
    chunks = k_ref.shape[0] // B_K_ROWS
    lane = lax.broadcasted_iota(jnp.int32, (1, LANES), 1)
    normed = []
    for hh in range(2):
        sl = slice(hh * HEAD_PAD, (hh + 1) * HEAD_PAD)
        q = q_ref[:, sl]

        def chunk(c, carry, sl=sl, q=q):
            m, acc = carry
            start = pl.multiple_of(c * B_K_ROWS, B_K_ROWS)
            s = _dot_nt(q, k_ref[pl.ds(start, B_K_ROWS), sl])
            m_new = jnp.maximum(m, jnp.max(s, axis=-1, keepdims=True))
            p = jnp.exp(s - m_new)
            acc = jnp.exp(m - m_new) * acc + _dot(p.astype(_BF16), v_ref[pl.ds(start, B_K_ROWS), sl])
            return m_new, acc

        _, acc = lax.fori_loop(0, chunks, chunk,
                               (jnp.full((rows, 1), NEG, _F32), jnp.zeros((rows, HEAD_PAD), _F32)))
        normed.append(acc / acc[:, V_HEAD_DIM:V_HEAD_DIM + 1])
    second = pltpu.roll(normed[1], V_HEAD_DIM, axis=1)
    o_ref[...] = jnp.where(lane < V_HEAD_DIM, normed[0], second).astype(_BF16)


def _mixer_b_call(qb, kb, vb):
    b, s, _ = qb.shape
    pairs = N_HEADS_B // 2
    kv_spec = pl.BlockSpec((None, s, 2 * HEAD_PAD), lambda bi, g, i: (bi, 0, g))
    return pl.pallas_call(
        _mixer_b_kernel,
        grid=(b, pairs, s // B_Q_ROWS),
        in_specs=[pl.BlockSpec((None, B_Q_ROWS, 2 * HEAD_PAD), lambda bi, g, i: (bi, i, g)), kv_spec, kv_spec],
        out_specs=pl.BlockSpec((None, B_Q_ROWS, 2 * V_HEAD_DIM), lambda bi, g, i: (bi, i, g)),
        out_shape=jax.ShapeDtypeStruct((b, s, WIDTH_B), _BF16),
        name="mixer_b",
        compiler_params=pltpu.CompilerParams(dimension_semantics=("parallel",) * 3,
                                             vmem_limit_bytes=VMEM_LIMIT_BYTES),
    )(qb, kb, vb)


def _merge_kernel(x_ref, o1_ref, o2_ref, o3_ref, l1_ref, l2_ref, l3_ref, ob_ref, g_a_ref, g_b_ref, w_o_ref, x1_ref):
    lses = [r[...] for r in (l1_ref, l2_ref, l3_ref)]
    top = jnp.maximum(jnp.maximum(lses[0], lses[1]), lses[2])
    wts = [jnp.exp(l - top) for l in lses]
    den = wts[0] + wts[1] + wts[2]
    o_a = sum(w * r[...].astype(_F32) for w, r in zip(wts, (o1_ref, o2_ref, o3_ref))) / den
    n_a = _rms(o_a, g_a_ref[...]).astype(_BF16)
    n_b = _rms(ob_ref[...].astype(_F32), g_b_ref[...]).astype(_BF16)
    x1_ref[...] = x_ref[...] + _dot(n_a, w_o_ref[:WIDTH_A, :]) + _dot(n_b, w_o_ref[WIDTH_A:, :])


def _merge_call(x, o_pats, lse_pats, o_b, g_a, g_b, w_o):
    b, s, _ = x.shape
    rows = PROJ_ROWS
    tok = lambda width: pl.BlockSpec((None, rows, width), lambda bi, i: (bi, i, 0))
    return pl.pallas_call(
        _merge_kernel,
        grid=(b, s // rows),
        in_specs=[tok(D_MODEL)] + [tok(WIDTH_A)] * 7
                 + [_const_spec(g_a.shape), _const_spec(g_b.shape), _const_spec(w_o.shape)],
        out_specs=tok(D_MODEL),
        out_shape=jax.ShapeDtypeStruct((b, s, D_MODEL), _F32),
        name="merge",
        compiler_params=pltpu.CompilerParams(dimension_semantics=("parallel", "parallel"),
                                             vmem_limit_bytes=VMEM_LIMIT_BYTES),
    )(x, *o_pats, *lse_pats, o_b, g_a, g_b, w_o)


def _ffn_kernel(x_ref, prev_ref, next_ref, g_ffn_ref, w_up_ref, w_conv_ref, b_conv_ref, w_down_ref, g_final_ref,
                y_ref):
    i, last = pl.program_id(1), pl.num_programs(1) - 1
    rows = x_ref.shape[0]
    g = g_ffn_ref[...]
    x = x_ref[...]
    h_prev = jnp.where(i > 0, _rms(prev_ref[...], g), 0.0)
    h_next = jnp.where(i < last, _rms(next_ref[...], g), 0.0)
    hn = jnp.concatenate([h_prev, _rms(x, g), h_next], axis=0).astype(_BF16)
    ext = rows + 2 * SUBLANES

    def conv(lo):
        u = _dot(hn, w_up_ref[:, lo:lo + FFN_COLS])
        w = w_conv_ref[:, lo:lo + FFN_COLS]
        before = pltpu.roll(u, 1, axis=0)[SUBLANES:SUBLANES + rows]
        after = pltpu.roll(u, ext - 1, axis=0)[SUBLANES:SUBLANES + rows]
        return (before * w[0:1] + u[SUBLANES:SUBLANES + rows] * w[1:2] + after * w[2:3]
                + b_conv_ref[:, lo:lo + FFN_COLS])

    acc = jnp.zeros((rows, D_MODEL), _F32)
    for c in range(D_FF // FFN_COLS):
        gate, val = conv(c * FFN_COLS), conv(D_FF + c * FFN_COLS)
        act = (gate * jax.nn.sigmoid(gate) * val).astype(_BF16)
        acc = acc + _dot(act, w_down_ref[c * FFN_COLS:(c + 1) * FFN_COLS, :])
    y_ref[...] = _rms(x + acc, g_final_ref[...])


def _ffn_call(x1, g_ffn, w_up, w_conv, b_conv, w_down, g_final):
    b, s, _ = x1.shape
    rows = PROJ_ROWS
    blocks = rows // SUBLANES
    last_block = s // SUBLANES - 1
    tok = pl.BlockSpec((None, rows, D_MODEL), lambda bi, i: (bi, i, 0))
    prev = pl.BlockSpec((None, SUBLANES, D_MODEL), lambda bi, i: (bi, jnp.maximum(i * blocks - 1, 0), 0))
    nxt = pl.BlockSpec((None, SUBLANES, D_MODEL), lambda bi, i: (bi, jnp.minimum((i + 1) * blocks, last_block), 0))
    return pl.pallas_call(
        _ffn_kernel,
        grid=(b, s // rows),
        in_specs=[tok, prev, nxt, _const_spec(g_ffn.shape), _const_spec(w_up.shape), _const_spec(w_conv.shape),
                  _const_spec(b_conv.shape), _const_spec(w_down.shape), _const_spec(g_final.shape)],
        out_specs=tok,
        out_shape=jax.ShapeDtypeStruct((b, s, D_MODEL), _F32),
        name="ffn",
        compiler_params=pltpu.CompilerParams(dimension_semantics=("parallel", "parallel"),
                                             vmem_limit_bytes=VMEM_LIMIT_BYTES),
    )(x1, x1, x1, g_ffn, w_up, w_conv, b_conv, w_down, g_final)


def _rotate_half_cols(w, head_dim):
    k, n = w.shape
    w3 = w.reshape(k, n // head_dim, head_dim)
    half = head_dim // 2
    return jnp.concatenate([-w3[..., half:], w3[..., :half]], axis=-1).reshape(k, n)


def _pad_heads(w, n_heads, lo, width):
    k = w.shape[0]
    w3 = w.reshape(k, n_heads, -1)[..., lo:lo + width]
    return jnp.pad(w3, ((0, 0), (0, 0), (0, HEAD_PAD - width))).reshape(k, n_heads * HEAD_PAD)


def _prepare_weights(w_in, w_uq, w_ukv):
    wq, wk, wv = (w_in[:, j * WIDTH_A:(j + 1) * WIDTH_A] for j in range(3))
    lo = 3 * WIDTH_A
    w_cq = w_in[:, lo:lo + Q_LORA_RANK]
    w_ckv = w_in[:, lo + Q_LORA_RANK:lo + Q_LORA_RANK + KV_LORA_RANK]
    w_kr = w_in[:, lo + Q_LORA_RANK + KV_LORA_RANK:]
    place = lambda w: jnp.pad(w, ((0, 0), (QK_NOPE_DIM, HEAD_PAD - QK_NOPE_DIM - QK_ROPE_DIM)))
    w_ext = jnp.concatenate(
        [wq, wk, _rotate_half_cols(wq, HEAD_DIM_A), _rotate_half_cols(wk, HEAD_DIM_A), wv, w_cq, w_ckv,
         place(w_kr), place(_rotate_half_cols(w_kr, QK_ROPE_DIM))], axis=1).astype(_BF16)
    assert w_ext.shape[1] == _EXT_WIDTH

    per_q = QK_NOPE_DIM + QK_ROPE_DIM
    uq3 = w_uq.reshape(Q_LORA_RANK, N_HEADS_B, per_q)
    rot = _rotate_half_cols(uq3[..., QK_NOPE_DIM:].reshape(Q_LORA_RANK, -1), QK_ROPE_DIM)
    rot = jnp.pad(rot.reshape(Q_LORA_RANK, N_HEADS_B, QK_ROPE_DIM),
                  ((0, 0), (0, 0), (QK_NOPE_DIM, HEAD_PAD - per_q))).reshape(Q_LORA_RANK, WIDTH_B_PAD)
    w_uq_ext = jnp.concatenate([_pad_heads(w_uq, N_HEADS_B, 0, per_q), rot], axis=1).astype(_BF16)

    w_ukv_ext = jnp.concatenate([_pad_heads(w_ukv, N_HEADS_B, 0, QK_NOPE_DIM),
                                 _pad_heads(w_ukv, N_HEADS_B, QK_NOPE_DIM, V_HEAD_DIM)], axis=1).astype(_BF16)
    return w_ext, w_uq_ext, w_ukv_ext


def _rope_tables(s):
    pos = jnp.arange(s, dtype=_F32)[:, None]

    def cos_sin(dim):
        half = dim // 2
        inv_freq = ROPE_THETA ** (-jnp.arange(half, dtype=_F32) * (2.0 / dim))
        ang = pos * inv_freq[None, :]
        return jnp.tile(jnp.cos(ang), (1, 2)), jnp.tile(jnp.sin(ang), (1, 2))

    cos_a, sin_a = (jnp.tile(t, (1, LANES // HEAD_DIM_A)) for t in cos_sin(HEAD_DIM_A))
    scale_a = HEAD_DIM_A ** -0.5
    tab_a = jnp.concatenate([cos_a * scale_a, sin_a * scale_a, cos_a, sin_a], axis=1)

    cos_b, sin_b = cos_sin(QK_ROPE_DIM)
    tail = HEAD_PAD - QK_NOPE_DIM - QK_ROPE_DIM
    lay = lambda nope, t: jnp.concatenate([jnp.full((s, QK_NOPE_DIM), nope, _F32), t, jnp.zeros((s, tail), _F32)], 1)
    scale_b = (QK_NOPE_DIM + QK_ROPE_DIM) ** -0.5
    tab_b = jnp.concatenate([lay(1.0, cos_b) * scale_b, lay(0.0, sin_b) * scale_b, lay(0.0, cos_b), lay(0.0, sin_b)],
                            axis=1)
    return tab_a, tab_b


def _layer(x, tabs, g_attn, w_in, g_q_lora, w_uq, g_kv_lora, w_ukv, g_out_a, g_out_b, w_o):
    row = lambda g: g.reshape(1, -1)
    w_ext, w_uq_ext, w_ukv_ext = _prepare_weights(w_in, w_uq, w_ukv)
    qa, ka, va, qb, kb, vb = _proj_call(x, *tabs, row(g_attn), w_ext, row(g_q_lora), w_uq_ext, row(g_kv_lora),
                                        w_ukv_ext)
    pats = [_mixer_a_call(qa, ka, va, dil) for _, dil in DILATED_CONFIGS]
    o_b = _mixer_b_call(qb, kb, vb)
    return _merge_call(x, [p[0] for p in pats], [p[1] for p in pats], o_b, row(g_out_a), row(g_out_b),
                       w_o.astype(_BF16))


def _trunk(x, g_attn, w_in, g_q_lora, w_uq, g_kv_lora, w_ukv, g_out_a, g_out_b, w_o, g_ffn, w_up, w_conv, b_conv,
           w_down, g_final):
    depth = w_in.shape[0]
    assert depth == 1, "the ffn kernel applies the final norm, so it closes the trunk"
    assert all(window // 2 // dil == A_RADIUS for window, dil in DILATED_CONFIGS)
    tabs = _rope_tables(x.shape[1])
    x1 = _layer(x, tabs, g_attn[0], w_in[0], g_q_lora[0], w_uq[0], g_kv_lora[0], w_ukv[0], g_out_a[0], g_out_b[0],
                w_o[0])
    return _ffn_call(x1, g_ffn[0].reshape(1, -1), w_up[0].astype(_BF16), w_conv[0], b_conv[0].reshape(1, -1),
                     w_down[0].astype(_BF16), g_final.reshape(1, -1))


def kernel(x_prompt, x_sample, g_attn, w_in, g_q_lora, w_uq, g_kv_lora, w_ukv, g_out_a, g_out_b, w_o, g_ffn, w_up,
           w_conv, b_conv, w_down, g_final):
    weights = (g_attn, w_in, g_q_lora, w_uq, g_kv_lora, w_ukv, g_out_a, g_out_b, w_o, g_ffn, w_up, w_conv, b_conv,
               w_down, g_final)
    return (_trunk(x_prompt, *weights), _trunk(x_sample, *weights))
```

```python
import functools

import jax
import jax.numpy as jnp
from jax import lax
from jax.experimental import pallas as pl
from jax.experimental.pallas import tpu as pltpu

D_MODEL = 1024
N_HEADS_A = 8
HEAD_DIM_A = 64
DILATED_CONFIGS = ((128, 1), (512, 4), (2048, 16))
N_HEADS_B = 8
Q_LORA_RANK = 256
KV_LORA_RANK = 128
QK_NOPE_DIM = 64
QK_ROPE_DIM = 32
V_HEAD_DIM = 64
WIDTH_A = N_HEADS_A * HEAD_DIM_A
WIDTH_B = N_HEADS_B * V_HEAD_DIM
D_FF = 2816
ROPE_THETA = 10000.0
EPS = 1e-6
NEG = -1e30
LOG2_E = 1.4426950408889634

LANES = 128
SUBLANES = 8
HEAD_PAD = LANES
WIDTH_B_PAD = N_HEADS_B * HEAD_PAD
VMEM_LIMIT_BYTES = 56 * 1024 * 1024

PROJ_ROWS = 512
A_ROWS = 128
A_UNROLL = 8
A_RADIUS = 64
B_Q_ROWS = 512
B_K_ROWS = 512
FFN_COLS = 256

_OFF_Q, _OFF_K, _OFF_QR, _OFF_KR, _OFF_V = 0, 512, 1024, 1536, 2048
_OFF_CQ = 2560
_OFF_CKV = _OFF_CQ + Q_LORA_RANK
_EXT_WIDTH = _OFF_CKV + 3 * LANES

_BF16 = jnp.bfloat16
_F32 = jnp.float32


def _rms(xf, g):
    return xf * lax.rsqrt(jnp.mean(xf * xf, axis=-1, keepdims=True) + EPS) * g


def _dot(a, b):
    return jnp.dot(a, b, preferred_element_type=_F32)


def _dot_nt(a, b):
    return lax.dot_general(a, b, (((1,), (1,)), ((), ())), preferred_element_type=_F32)


def _proj_kernel(x_ref, tab_a_ref, tab_b_ref, g_attn_ref, w_ext_ref, g_q_ref, w_uq_ref, g_kv_ref, w_ukv_ref,
                 qa_ref, ka_ref, va_ref, qb_ref, kb_ref, vb_ref):
    hn = _rms(x_ref[...], g_attn_ref[...]).astype(_BF16)

    def proj(lo, width):
        return _dot(hn, w_ext_ref[:, lo:lo + width])

    def tab(ref, j):
        return ref[:, j * LANES:(j + 1) * LANES]

    for off_main, off_rot, j, out_ref in ((_OFF_Q, _OFF_QR, 0, qa_ref), (_OFF_K, _OFF_KR, 2, ka_ref)):
        main, rot = proj(off_main, WIDTH_A), proj(off_rot, WIDTH_A)
        cos, sin = tab(tab_a_ref, j), tab(tab_a_ref, j + 1)
        for g in range(WIDTH_A // LANES):
            sl = slice(g * LANES, (g + 1) * LANES)
            out_ref[:, sl] = (main[:, sl] * cos + rot[:, sl] * sin).astype(_BF16)
    va_ref[...] = proj(_OFF_V, WIDTH_A).astype(_BF16)

    cqn = _rms(proj(_OFF_CQ, Q_LORA_RANK), g_q_ref[...]).astype(_BF16)
    qb = _dot(cqn, w_uq_ref[...])
    cos, sin = tab(tab_b_ref, 0), tab(tab_b_ref, 1)
    for h in range(N_HEADS_B):
        sl = slice(h * HEAD_PAD, (h + 1) * HEAD_PAD)
        rot = qb[:, WIDTH_B_PAD + h * HEAD_PAD:WIDTH_B_PAD + (h + 1) * HEAD_PAD]
        qb_ref[:, sl] = (qb[:, sl] * cos + rot * sin).astype(_BF16)

    rest = proj(_OFF_CKV, 3 * LANES)
    ckvn = _rms(rest[:, :KV_LORA_RANK], g_kv_ref[...]).astype(_BF16)
    kv = _dot(ckvn, w_ukv_ref[...])
    k_rot = rest[:, LANES:2 * LANES] * tab(tab_b_ref, 2) + rest[:, 2 * LANES:] * tab(tab_b_ref, 3)
    lane = lax.broadcasted_iota(jnp.int32, (1, HEAD_PAD), 1)
    one_col = (lane == V_HEAD_DIM).astype(_F32)
    for h in range(N_HEADS_B):
        sl = slice(h * HEAD_PAD, (h + 1) * HEAD_PAD)
        kb_ref[:, sl] = (kv[:, sl] + k_rot).astype(_BF16)
        vb_ref[:, sl] = (kv[:, WIDTH_B_PAD + h * HEAD_PAD:WIDTH_B_PAD + (h + 1) * HEAD_PAD] + one_col).astype(_BF16)


def _const_spec(shape):
    return pl.BlockSpec(shape, lambda *_: (0,) * len(shape))


def _proj_call(x, tab_a, tab_b, g_attn, w_ext, g_q, w_uq, g_kv, w_ukv):
    b, s, _ = x.shape
    rows = PROJ_ROWS
    tok = lambda width: pl.BlockSpec((None, rows, width), lambda bi, i: (bi, i, 0))
    tabs = lambda width: pl.BlockSpec((rows, width), lambda bi, i: (i, 0))
    out_a = jax.ShapeDtypeStruct((b, s, WIDTH_A), _BF16)
    out_b = jax.ShapeDtypeStruct((b, s, WIDTH_B_PAD), _BF16)
    return pl.pallas_call(
        _proj_kernel,
        grid=(b, s // rows),
        in_specs=[tok(D_MODEL), tabs(4 * LANES), tabs(4 * LANES),
                  _const_spec(g_attn.shape), _const_spec(w_ext.shape), _const_spec(g_q.shape),
                  _const_spec(w_uq.shape), _const_spec(g_kv.shape), _const_spec(w_ukv.shape)],
        out_specs=[tok(WIDTH_A)] * 3 + [tok(WIDTH_B_PAD)] * 3,
        out_shape=[out_a] * 3 + [out_b] * 3,
        name="proj",
        compiler_params=pltpu.CompilerParams(dimension_semantics=("parallel", "parallel"),
                                             vmem_limit_bytes=VMEM_LIMIT_BYTES),
    )(x, tab_a, tab_b, g_attn, w_ext, g_q, w_uq, g_kv, w_ukv)


def _mixer_a_kernel(q_ref, k_ref, v_ref, o_ref, lse_ref):
    length = q_ref.shape[0]
    window = A_ROWS + 2 * A_RADIUS
    lane = lax.broadcasted_iota(jnp.int32, (1, LANES), 1)
    first_head = lane < HEAD_DIM_A
    col_minus_row = (lax.broadcasted_iota(jnp.int32, (A_ROWS, window), 1)
                     - lax.broadcasted_iota(jnp.int32, (A_ROWS, window), 0))

    def tile(t, carry):
        a0 = pl.multiple_of(t * A_ROWS, A_ROWS)
        ws = pl.multiple_of(jnp.clip(a0 - A_RADIUS, 0, length - window), A_RADIUS)
        valid = jnp.abs(col_minus_row + (ws - a0)) <= A_RADIUS
        q = q_ref[pl.ds(a0, A_ROWS), :]
        kw = k_ref[pl.ds(ws, window), :]
        vw = v_ref[pl.ds(ws, window), :]
        outs, lses = [], []
        for head_sel in (first_head, jnp.logical_not(first_head)):
            s = _dot_nt(jnp.where(head_sel, q, jnp.zeros_like(q)), kw)
            s = jnp.where(valid, s, NEG)
            m = jnp.max(s, axis=-1, keepdims=True)
            p = jnp.exp(s - m)
            den = jnp.sum(p, axis=-1, keepdims=True)
            outs.append(_dot(p.astype(_BF16), vw) / den)
            lses.append(m + jnp.log(den))
        o_ref[pl.ds(a0, A_ROWS), :] = jnp.where(first_head, outs[0], outs[1]).astype(_BF16)
        lse_ref[pl.ds(a0, A_ROWS), :] = jnp.where(first_head, lses[0], lses[1])
        return carry

    lax.fori_loop(0, length // A_ROWS, tile, 0, unroll=A_UNROLL)


def _mixer_a_call(qa, ka, va, dil):
    b, s, _ = qa.shape
    length = s // dil
    assert length % A_ROWS == 0 and length >= A_ROWS + 2 * A_RADIUS
    groups = WIDTH_A // LANES
    view = lambda t: t.reshape(b, length, dil * WIDTH_A)
    spec = pl.BlockSpec((None, length, LANES), lambda bi, r, g: (bi, 0, r * groups + g))
    o, lse = pl.pallas_call(
        _mixer_a_kernel,
        grid=(b, dil, groups),
        in_specs=[spec] * 3,
        out_specs=[spec] * 2,
        out_shape=[jax.ShapeDtypeStruct((b, length, dil * WIDTH_A), _BF16),
                   jax.ShapeDtypeStruct((b, length, dil * WIDTH_A), _F32)],
        name=f"mixer_a_dil{dil}",
        compiler_params=pltpu.CompilerParams(dimension_semantics=("parallel",) * 3,
                                             vmem_limit_bytes=VMEM_LIMIT_BYTES),
    )(view(qa), view(ka), view(va))
    return o.reshape(b, s, WIDTH_A), lse.reshape(b, s, WIDTH_A)


def _mixer_b_kernel(q_ref, k_ref, v_ref, o_ref, s_even, s_odd, p_even, p_odd):
    rows = q_ref.shape[0]
    chunks = k_ref.shape[0] // B_K_ROWS
    assert chunks % 2 == 0 and chunks >= 4
    heads = [slice(hh * HEAD_PAD, (hh + 1) * HEAD_PAD) for hh in range(2)]

    def rows_of(c):
        return pl.ds(pl.multiple_of(c * B_K_ROWS, B_K_ROWS), B_K_ROWS)

    def scores(c, s_ref):
        for hh, sl in enumerate(heads):
            s_ref[hh] = _dot_nt(q_ref[:, sl], k_ref[rows_of(c), sl])

    def softmax(s_ref, p_ref, maxes):
        out = []
        for hh, m in enumerate(maxes):
            s = s_ref[hh]
            m_new = jnp.maximum(m, jnp.max(s, axis=-1, keepdims=True))
            p_ref[hh] = jnp.exp2(s - m_new).astype(_BF16)
            out.append((m_new, jnp.exp2(m - m_new)))
        return [o[0] for o in out], [o[1] for o in out]

    def values(c, p_ref, alphas, accs):
        return [alpha * acc + _dot(p_ref[hh], v_ref[rows_of(c), sl])
                for hh, (sl, alpha, acc) in enumerate(zip(heads, alphas, accs))]

    maxes = [jnp.full((rows, 1), NEG, _F32)] * 2
    accs = [jnp.zeros((rows, HEAD_PAD), _F32)] * 2
    scores(0, s_even)
    scores(1, s_odd)
    maxes, alphas = softmax(s_even, p_even, maxes)

    def pair(j, carry):
        maxes, alphas, accs = carry
        c = 2 * j + 1
        scores(c + 1, s_even)
        maxes, alphas_odd = softmax(s_odd, p_odd, maxes)
        accs = values(c - 1, p_even, alphas, accs)
        scores(c + 2, s_odd)
        maxes, alphas = softmax(s_even, p_even, maxes)
        accs = values(c, p_odd, alphas_odd, accs)
        return maxes, alphas, accs

    maxes, alphas, accs = lax.fori_loop(0, chunks // 2 - 1, pair, (maxes, alphas, accs), unroll=True)
    maxes, alphas_odd = softmax(s_odd, p_odd, maxes)
    accs = values(chunks - 2, p_even, alphas, accs)
    accs = values(chunks - 1, p_odd, alphas_odd, accs)

    lane = lax.broadcasted_iota(jnp.int32, (1, LANES), 1)
    normed = [acc / acc[:, V_HEAD_DIM:V_HEAD_DIM + 1] for acc in accs]
    second = pltpu.roll(normed[1], V_HEAD_DIM, axis=1)
    o_ref[...] = jnp.where(lane < V_HEAD_DIM, normed[0], second).astype(_BF16)


def _mixer_b_call(qb, kb, vb):
    b, s, _ = qb.shape
    pairs = N_HEADS_B // 2
    kv_spec = pl.BlockSpec((None, s, 2 * HEAD_PAD), lambda bi, g, i: (bi, 0, g))
    return pl.pallas_call(
        _mixer_b_kernel,
        grid=(b, pairs, s // B_Q_ROWS),
        in_specs=[pl.BlockSpec((None, B_Q_ROWS, 2 * HEAD_PAD), lambda bi, g, i: (bi, i, g)), kv_spec, kv_spec],
        out_specs=pl.BlockSpec((None, B_Q_ROWS, 2 * V_HEAD_DIM), lambda bi, g, i: (bi, i, g)),
        out_shape=jax.ShapeDtypeStruct((b, s, WIDTH_B), _BF16),
        scratch_shapes=[pltpu.VMEM((2, B_Q_ROWS, B_K_ROWS), _F32)] * 2
                       + [pltpu.VMEM((2, B_Q_ROWS, B_K_ROWS), _BF16)] * 2,
        name="mixer_b",
        compiler_params=pltpu.CompilerParams(dimension_semantics=("parallel",) * 3,
                                             vmem_limit_bytes=VMEM_LIMIT_BYTES),
    )(qb, kb, vb)


def _merge_kernel(x_ref, o1_ref, o2_ref, o3_ref, l1_ref, l2_ref, l3_ref, ob_ref, g_a_ref, g_b_ref, w_o_ref, x1_ref):
    lses = [r[...] for r in (l1_ref, l2_ref, l3_ref)]
    top = jnp.maximum(jnp.maximum(lses[0], lses[1]), lses[2])
    wts = [jnp.exp(l - top) for l in lses]
    den = wts[0] + wts[1] + wts[2]
    o_a = sum(w * r[...].astype(_F32) for w, r in zip(wts, (o1_ref, o2_ref, o3_ref))) / den
    n_a = _rms(o_a, g_a_ref[...]).astype(_BF16)
    n_b = _rms(ob_ref[...].astype(_F32), g_b_ref[...]).astype(_BF16)
    x1_ref[...] = x_ref[...] + _dot(n_a, w_o_ref[:WIDTH_A, :]) + _dot(n_b, w_o_ref[WIDTH_A:, :])


def _merge_call(x, o_pats, lse_pats, o_b, g_a, g_b, w_o):
    b, s, _ = x.shape
    rows = PROJ_ROWS
    tok = lambda width: pl.BlockSpec((None, rows, width), lambda bi, i: (bi, i, 0))
    return pl.pallas_call(
        _merge_kernel,
        grid=(b, s // rows),
        in_specs=[tok(D_MODEL)] + [tok(WIDTH_A)] * 7
                 + [_const_spec(g_a.shape), _const_spec(g_b.shape), _const_spec(w_o.shape)],
        out_specs=tok(D_MODEL),
        out_shape=jax.ShapeDtypeStruct((b, s, D_MODEL), _F32),
        name="merge",
        compiler_params=pltpu.CompilerParams(dimension_semantics=("parallel", "parallel"),
                                             vmem_limit_bytes=VMEM_LIMIT_BYTES),
    )(x, *o_pats, *lse_pats, o_b, g_a, g_b, w_o)


def _ffn_kernel(x_ref, prev_ref, next_ref, g_ffn_ref, w_up_ref, w_conv_ref, b_conv_ref, w_down_ref, g_final_ref,
                y_ref):
    i, last = pl.program_id(1), pl.num_programs(1) - 1
    rows = x_ref.shape[0]
    g = g_ffn_ref[...]
    x = x_ref[...]
    h_prev = jnp.where(i > 0, _rms(prev_ref[...], g), 0.0)
    h_next = jnp.where(i < last, _rms(next_ref[...], g), 0.0)
    hn = jnp.concatenate([h_prev, _rms(x, g), h_next], axis=0).astype(_BF16)
    ext = rows + 2 * SUBLANES

    def conv(lo):
        u = _dot(hn, w_up_ref[:, lo:lo + FFN_COLS])
        w = w_conv_ref[:, lo:lo + FFN_COLS]
        before = pltpu.roll(u, 1, axis=0)[SUBLANES:SUBLANES + rows]
        after = pltpu.roll(u, ext - 1, axis=0)[SUBLANES:SUBLANES + rows]
        return (before * w[0:1] + u[SUBLANES:SUBLANES + rows] * w[1:2] + after * w[2:3]
                + b_conv_ref[:, lo:lo + FFN_COLS])

    acc = jnp.zeros((rows, D_MODEL), _F32)
    for c in range(D_FF // FFN_COLS):
        gate, val = conv(c * FFN_COLS), conv(D_FF + c * FFN_COLS)
        act = (gate * jax.nn.sigmoid(gate) * val).astype(_BF16)
        acc = acc + _dot(act, w_down_ref[c * FFN_COLS:(c + 1) * FFN_COLS, :])
    y_ref[...] = _rms(x + acc, g_final_ref[...])


def _ffn_call(x1, g_ffn, w_up, w_conv, b_conv, w_down, g_final):
    b, s, _ = x1.shape
    rows = PROJ_ROWS
    blocks = rows // SUBLANES
    last_block = s // SUBLANES - 1
    tok = pl.BlockSpec((None, rows, D_MODEL), lambda bi, i: (bi, i, 0))
    prev = pl.BlockSpec((None, SUBLANES, D_MODEL), lambda bi, i: (bi, jnp.maximum(i * blocks - 1, 0), 0))
    nxt = pl.BlockSpec((None, SUBLANES, D_MODEL), lambda bi, i: (bi, jnp.minimum((i + 1) * blocks, last_block), 0))
    return pl.pallas_call(
        _ffn_kernel,
        grid=(b, s // rows),
        in_specs=[tok, prev, nxt, _const_spec(g_ffn.shape), _const_spec(w_up.shape), _const_spec(w_conv.shape),
                  _const_spec(b_conv.shape), _const_spec(w_down.shape), _const_spec(g_final.shape)],
        out_specs=tok,
        out_shape=jax.ShapeDtypeStruct((b, s, D_MODEL), _F32),
        name="ffn",
        compiler_params=pltpu.CompilerParams(dimension_semantics=("parallel", "parallel"),
                                             vmem_limit_bytes=VMEM_LIMIT_BYTES),
    )(x1, x1, x1, g_ffn, w_up, w_conv, b_conv, w_down, g_final)


def _rotate_half_cols(w, head_dim):
    k, n = w.shape
    w3 = w.reshape(k, n // head_dim, head_dim)
    half = head_dim // 2
    return jnp.concatenate([-w3[..., half:], w3[..., :half]], axis=-1).reshape(k, n)


def _pad_heads(w, n_heads, lo, width):
    k = w.shape[0]
    w3 = w.reshape(k, n_heads, -1)[..., lo:lo + width]
    return jnp.pad(w3, ((0, 0), (0, 0), (0, HEAD_PAD - width))).reshape(k, n_heads * HEAD_PAD)


def _prepare_weights(w_in, w_uq, w_ukv):
    wq, wk, wv = (w_in[:, j * WIDTH_A:(j + 1) * WIDTH_A] for j in range(3))
    lo = 3 * WIDTH_A
    w_cq = w_in[:, lo:lo + Q_LORA_RANK]
    w_ckv = w_in[:, lo + Q_LORA_RANK:lo + Q_LORA_RANK + KV_LORA_RANK]
    w_kr = w_in[:, lo + Q_LORA_RANK + KV_LORA_RANK:]
    place = lambda w: jnp.pad(w, ((0, 0), (QK_NOPE_DIM, HEAD_PAD - QK_NOPE_DIM - QK_ROPE_DIM)))
    w_ext = jnp.concatenate(
        [wq, wk, _rotate_half_cols(wq, HEAD_DIM_A), _rotate_half_cols(wk, HEAD_DIM_A), wv, w_cq, w_ckv,
         place(w_kr), place(_rotate_half_cols(w_kr, QK_ROPE_DIM))], axis=1).astype(_BF16)
    assert w_ext.shape[1] == _EXT_WIDTH

    per_q = QK_NOPE_DIM + QK_ROPE_DIM
    uq3 = w_uq.reshape(Q_LORA_RANK, N_HEADS_B, per_q)
    rot = _rotate_half_cols(uq3[..., QK_NOPE_DIM:].reshape(Q_LORA_RANK, -1), QK_ROPE_DIM)
    rot = jnp.pad(rot.reshape(Q_LORA_RANK, N_HEADS_B, QK_ROPE_DIM),
                  ((0, 0), (0, 0), (QK_NOPE_DIM, HEAD_PAD - per_q))).reshape(Q_LORA_RANK, WIDTH_B_PAD)
    w_uq_ext = jnp.concatenate([_pad_heads(w_uq, N_HEADS_B, 0, per_q), rot], axis=1).astype(_BF16)

    w_ukv_ext = jnp.concatenate([_pad_heads(w_ukv, N_HEADS_B, 0, QK_NOPE_DIM),
                                 _pad_heads(w_ukv, N_HEADS_B, QK_NOPE_DIM, V_HEAD_DIM)], axis=1).astype(_BF16)
    return w_ext, w_uq_ext, w_ukv_ext


def _rope_tables(s):
    pos = jnp.arange(s, dtype=_F32)[:, None]

    def cos_sin(dim):
        half = dim // 2
        inv_freq = ROPE_THETA ** (-jnp.arange(half, dtype=_F32) * (2.0 / dim))
        ang = pos * inv_freq[None, :]
        return jnp.tile(jnp.cos(ang), (1, 2)), jnp.tile(jnp.sin(ang), (1, 2))

    cos_a, sin_a = (jnp.tile(t, (1, LANES // HEAD_DIM_A)) for t in cos_sin(HEAD_DIM_A))
    scale_a = HEAD_DIM_A ** -0.5
    tab_a = jnp.concatenate([cos_a * scale_a, sin_a * scale_a, cos_a, sin_a], axis=1)

    cos_b, sin_b = cos_sin(QK_ROPE_DIM)
    tail = HEAD_PAD - QK_NOPE_DIM - QK_ROPE_DIM
    lay = lambda nope, t: jnp.concatenate([jnp.full((s, QK_NOPE_DIM), nope, _F32), t, jnp.zeros((s, tail), _F32)], 1)
    scale_b = (QK_NOPE_DIM + QK_ROPE_DIM) ** -0.5 * LOG2_E
    tab_b = jnp.concatenate([lay(1.0, cos_b) * scale_b, lay(0.0, sin_b) * scale_b, lay(0.0, cos_b), lay(0.0, sin_b)],
                            axis=1)
    return tab_a, tab_b


def _layer(x, tabs, g_attn, w_in, g_q_lora, w_uq, g_kv_lora, w_ukv, g_out_a, g_out_b, w_o):
    row = lambda g: g.reshape(1, -1)
    w_ext, w_uq_ext, w_ukv_ext = _prepare_weights(w_in, w_uq, w_ukv)
    qa, ka, va, qb, kb, vb = _proj_call(x, *tabs, row(g_attn), w_ext, row(g_q_lora), w_uq_ext, row(g_kv_lora),
                                        w_ukv_ext)
    pats = [_mixer_a_call(qa, ka, va, dil) for _, dil in DILATED_CONFIGS]
    o_b = _mixer_b_call(qb, kb, vb)
    return _merge_call(x, [p[0] for p in pats], [p[1] for p in pats], o_b, row(g_out_a), row(g_out_b),
                       w_o.astype(_BF16))


def _trunk(x, g_attn, w_in, g_q_lora, w_uq, g_kv_lora, w_ukv, g_out_a, g_out_b, w_o, g_ffn, w_up, w_conv, b_conv,
           w_down, g_final):
    depth = w_in.shape[0]
    assert depth == 1, "the ffn kernel applies the final norm, so it closes the trunk"
    assert all(window // 2 // dil == A_RADIUS for window, dil in DILATED_CONFIGS)
    tabs = _rope_tables(x.shape[1])
    x1 = _layer(x, tabs, g_attn[0], w_in[0], g_q_lora[0], w_uq[0], g_kv_lora[0], w_ukv[0], g_out_a[0], g_out_b[0],
                w_o[0])
    return _ffn_call(x1, g_ffn[0].reshape(1, -1), w_up[0].astype(_BF16), w_conv[0], b_conv[0].reshape(1, -1),
                     w_down[0].astype(_BF16), g_final.reshape(1, -1))


def kernel(x_prompt, x_sample, g_attn, w_in, g_q_lora, w_uq, g_kv_lora, w_ukv, g_out_a, g_out_b, w_o, g_ffn, w_up,
           w_conv, b_conv, w_down, g_final):
    weights = (g_attn, w_in, g_q_lora, w_uq, g_kv_lora, w_ukv, g_out_a, g_out_b, w_o, g_ffn, w_up, w_conv, b_conv,
               w_down, g_final)
    return (_trunk(x_prompt, *weights), _trunk(x_sample, *weights))
```

```python
import jax
import jax.numpy as jnp
from jax import lax
from jax.experimental import pallas as pl
from jax.experimental.pallas import tpu as pltpu

D_MODEL = 1024
N_HEADS_A = 8
HEAD_DIM_A = 64
DILATED_CONFIGS = ((128, 1), (512, 4), (2048, 16))
N_HEADS_B = 8
Q_LORA_RANK = 256
KV_LORA_RANK = 128
QK_NOPE_DIM = 64
QK_ROPE_DIM = 32
V_HEAD_DIM = 64
WIDTH_A = N_HEADS_A * HEAD_DIM_A
WIDTH_B = N_HEADS_B * V_HEAD_DIM
D_FF = 2816
ROPE_THETA = 10000.0
EPS = 1e-6
NEG = -1e30
LOG2_E = 1.4426950408889634

LANES = 128
SUBLANES = 8
HEAD_PAD = LANES
WIDTH_B_PAD = N_HEADS_B * HEAD_PAD
GROUPS_A = WIDTH_A // LANES
VMEM_LIMIT_BYTES = 56 * 1024 * 1024

DILATIONS = tuple(dil for _, dil in DILATED_CONFIGS)
PROJ_ROWS = 512
A_ROWS = 128
A_UNROLL = 8
A_RADIUS = 64
A_WINDOW = A_ROWS + 2 * A_RADIUS
B_Q_ROWS = 512
B_K_ROWS = 512
FFN_COLS = 256
FFN_SPLIT = 6

_OFF_Q, _OFF_K, _OFF_QR, _OFF_KR, _OFF_V = 0, 512, 1024, 1536, 2048
_OFF_CQ = 2560
_OFF_CKV = _OFF_CQ + Q_LORA_RANK
_EXT_WIDTH = _OFF_CKV + 3 * LANES

_BF16 = jnp.bfloat16
_F32 = jnp.float32


def _rms(xf, g):
    return xf * lax.rsqrt(jnp.mean(xf * xf, axis=-1, keepdims=True) + EPS) * g


def _dot(a, b):
    return jnp.dot(a, b, preferred_element_type=_F32)


def _dot_nt(a, b):
    return lax.dot_general(a, b, (((1,), (1,)), ((), ())), preferred_element_type=_F32)


def _group(g):
    return slice(g * LANES, (g + 1) * LANES)


def _const_spec(shape):
    return pl.BlockSpec(shape, lambda *_: (0,) * len(shape))


def _params(n_axes):
    return pltpu.CompilerParams(dimension_semantics=("parallel",) * n_axes, vmem_limit_bytes=VMEM_LIMIT_BYTES)


def _proj_kernel(x_ref, tab_a_ref, tab_b_ref, g_attn_ref, w_ext_ref, g_q_ref, w_uq_ref, g_kv_ref, w_ukv_ref, *refs):
    n_dil = len(DILATIONS)
    a_refs = [refs[3 * j:3 * j + 3] for j in range(n_dil)]
    qb_ref, kb_ref, vb_ref = refs[3 * n_dil:3 * n_dil + 3]
    stage_refs = refs[3 * n_dil + 3:]
    rows = x_ref.shape[0]
    hn = _rms(x_ref[...], g_attn_ref[...]).astype(_BF16)

    def proj(lo, width):
        return _dot(hn, w_ext_ref[:, lo:lo + width])

    def tab(ref, j):
        return ref[:, _group(j)]

    for j, (off_main, off_rot) in enumerate(((_OFF_Q, _OFF_QR), (_OFF_K, _OFF_KR), (_OFF_V, None))):
        stage_ref = stage_refs[j]
        main = proj(off_main, WIDTH_A)
        if off_rot is None:
            for g in range(GROUPS_A):
                stage_ref[g] = main[:, _group(g)]
        else:
            rot = proj(off_rot, WIDTH_A)
            for g in range(GROUPS_A):
                stage_ref[g] = main[:, _group(g)] * tab(tab_a_ref, 2 * j) + rot[:, _group(g)] * tab(tab_a_ref, 2 * j + 1)
        for outs, dil in zip(a_refs, DILATIONS):
            for g in range(GROUPS_A):
                for r in range(dil):
                    rows_r = pl.ds(r, rows // dil, stride=dil) if dil > 1 else pl.ds(0, rows)
                    outs[j][r, :, _group(g)] = stage_ref[g, rows_r, :].astype(_BF16)

    cqn = _rms(proj(_OFF_CQ, Q_LORA_RANK), g_q_ref[...]).astype(_BF16)
    qb = _dot(cqn, w_uq_ref[...])
    cos, sin = tab(tab_b_ref, 0), tab(tab_b_ref, 1)
    for h in range(N_HEADS_B):
        rot = qb[:, WIDTH_B_PAD + h * HEAD_PAD:WIDTH_B_PAD + (h + 1) * HEAD_PAD]
        qb_ref[:, _group(h)] = (qb[:, _group(h)] * cos + rot * sin).astype(_BF16)

    rest = proj(_OFF_CKV, 3 * LANES)
    ckvn = _rms(rest[:, :KV_LORA_RANK], g_kv_ref[...]).astype(_BF16)
    kv = _dot(ckvn, w_ukv_ref[...])
    kb_rot = rest[:, LANES:2 * LANES] * tab(tab_b_ref, 2) + rest[:, 2 * LANES:] * tab(tab_b_ref, 3)
    lane = lax.broadcasted_iota(jnp.int32, (1, HEAD_PAD), 1)
    one_col = (lane == V_HEAD_DIM).astype(_F32)
    for h in range(N_HEADS_B):
        kb_ref[:, _group(h)] = (kv[:, _group(h)] + kb_rot).astype(_BF16)
        vb_ref[:, _group(h)] = (kv[:, WIDTH_B_PAD + h * HEAD_PAD:WIDTH_B_PAD + (h + 1) * HEAD_PAD]
                                + one_col).astype(_BF16)


def _residue_spec(dil, rows):
    return pl.BlockSpec((None, dil, rows // dil, WIDTH_A), lambda bi, i: (bi, 0, i, 0))


def _proj_call(x, tab_a, tab_b, g_attn, w_ext, g_q, w_uq, g_kv, w_ukv):
    b, s, _ = x.shape
    rows = PROJ_ROWS
    tok = lambda width: pl.BlockSpec((None, rows, width), lambda bi, i: (bi, i, 0))
    tabs = lambda width: pl.BlockSpec((rows, width), lambda bi, i: (i, 0))
    out_b = jax.ShapeDtypeStruct((b, s, WIDTH_B_PAD), _BF16)
    a_specs = [_residue_spec(dil, rows) for dil in DILATIONS for _ in range(3)]
    a_shapes = [jax.ShapeDtypeStruct((b, dil, s // dil, WIDTH_A), _BF16) for dil in DILATIONS for _ in range(3)]
    outs = pl.pallas_call(
        _proj_kernel,
        grid=(b, s // rows),
        in_specs=[tok(D_MODEL), tabs(4 * LANES), tabs(4 * LANES),
                  _const_spec(g_attn.shape), _const_spec(w_ext.shape), _const_spec(g_q.shape),
                  _const_spec(w_uq.shape), _const_spec(g_kv.shape), _const_spec(w_ukv.shape)],
        out_specs=a_specs + [tok(WIDTH_B_PAD)] * 3,
        out_shape=a_shapes + [out_b] * 3,
        scratch_shapes=[pltpu.VMEM((GROUPS_A, rows, LANES), _F32)] * 3,
        name="proj",
        compiler_params=_params(2),
    )(x, tab_a, tab_b, g_attn, w_ext, g_q, w_uq, g_kv, w_ukv)
    n = 3 * len(DILATIONS)
    return [outs[3 * j:3 * j + 3] for j in range(len(DILATIONS))], outs[n:]


def _mixer_a_kernel(q_ref, k_ref, v_ref, o_ref, lse_ref, bias_ref):
    length = q_ref.shape[0]
    lane = lax.broadcasted_iota(jnp.int32, (1, LANES), 1)
    first_head = lane < HEAD_DIM_A
    col_minus_row = (lax.broadcasted_iota(jnp.int32, (A_ROWS, A_WINDOW), 1)
                     - lax.broadcasted_iota(jnp.int32, (A_ROWS, A_WINDOW), 0))
    for case in range(3):
        bias_ref[case] = jnp.where(jnp.abs(col_minus_row - case * A_RADIUS) <= A_RADIUS, 0.0, NEG)

    def tile(t, carry):
        a0 = pl.multiple_of(t * A_ROWS, A_ROWS)
        ws = pl.multiple_of(jnp.clip(a0 - A_RADIUS, 0, length - A_WINDOW), A_RADIUS)
        bias = bias_ref[(a0 - ws) // A_RADIUS]
        q = q_ref[pl.ds(a0, A_ROWS), :]
        kw = k_ref[pl.ds(ws, A_WINDOW), :]
        vw = v_ref[pl.ds(ws, A_WINDOW), :]
        outs, lses = [], []
        for head_sel in (first_head, jnp.logical_not(first_head)):
            s = _dot_nt(jnp.where(head_sel, q, jnp.zeros_like(q)), kw) + bias
            m = jnp.max(s, axis=-1, keepdims=True)
            p = jnp.exp2(s - m)
            den = jnp.sum(p, axis=-1, keepdims=True)
            outs.append(_dot(p.astype(_BF16), vw) / den)
            lses.append(m + jnp.log2(den))
        o_ref[pl.ds(a0, A_ROWS), :] = jnp.where(first_head, outs[0], outs[1]).astype(_BF16)
        lse_ref[pl.ds(a0, A_ROWS), :] = jnp.where(first_head, lses[0], lses[1])
        return carry

    lax.fori_loop(0, length // A_ROWS, tile, 0, unroll=A_UNROLL)


def _mixer_a_call(qa, ka, va, dil):
    b, _, length, _ = qa.shape
    assert length % A_ROWS == 0 and length >= A_WINDOW
    spec = pl.BlockSpec((None, None, length, LANES), lambda bi, r, g: (bi, r, 0, g))
    return pl.pallas_call(
        _mixer_a_kernel,
        grid=(b, dil, GROUPS_A),
        in_specs=[spec] * 3,
        out_specs=[spec] * 2,
        out_shape=[jax.ShapeDtypeStruct(qa.shape, _BF16), jax.ShapeDtypeStruct(qa.shape, _F32)],
        scratch_shapes=[pltpu.VMEM((3, A_ROWS, A_WINDOW), _F32)],
        name=f"mixer_a_dil{dil}",
        compiler_params=_params(3),
    )(qa, ka, va)


def _mixer_b_kernel(q_ref, k_ref, v_ref, o_ref, s_even, s_odd, p_even, p_odd):
    rows = q_ref.shape[0]
    chunks = k_ref.shape[0] // B_K_ROWS
    assert chunks % 2 == 0 and chunks >= 4
    heads = [_group(hh) for hh in range(2)]

    def rows_of(c):
        return pl.ds(pl.multiple_of(c * B_K_ROWS, B_K_ROWS), B_K_ROWS)

    def scores(c, s_ref):
        for hh, sl in enumerate(heads):
            s_ref[hh] = _dot_nt(q_ref[:, sl], k_ref[rows_of(c), sl])

    def softmax(s_ref, p_ref, maxes):
        out = []
        for hh, m in enumerate(maxes):
            s = s_ref[hh]
            m_new = jnp.maximum(m, jnp.max(s, axis=-1, keepdims=True))
            p_ref[hh] = jnp.exp2(s - m_new).astype(_BF16)
            out.append((m_new, jnp.exp2(m - m_new)))
        return [o[0] for o in out], [o[1] for o in out]

    def values(c, p_ref, alphas, accs):
        return [alpha * acc + _dot(p_ref[hh], v_ref[rows_of(c), sl])
                for hh, (sl, alpha, acc) in enumerate(zip(heads, alphas, accs))]

    maxes = [jnp.full((rows, 1), NEG, _F32)] * 2
    accs = [jnp.zeros((rows, HEAD_PAD), _F32)] * 2
    scores(0, s_even)
    scores(1, s_odd)
    maxes, alphas = softmax(s_even, p_even, maxes)

    def pair(j, carry):
        maxes, alphas, accs = carry
        c = 2 * j + 1
        scores(c + 1, s_even)
        maxes, alphas_odd = softmax(s_odd, p_odd, maxes)
        accs = values(c - 1, p_even, alphas, accs)
        scores(c + 2, s_odd)
        maxes, alphas = softmax(s_even, p_even, maxes)
        accs = values(c, p_odd, alphas_odd, accs)
        return maxes, alphas, accs

    maxes, alphas, accs = lax.fori_loop(0, chunks // 2 - 1, pair, (maxes, alphas, accs), unroll=True)
    maxes, alphas_odd = softmax(s_odd, p_odd, maxes)
    accs = values(chunks - 2, p_even, alphas, accs)
    accs = values(chunks - 1, p_odd, alphas_odd, accs)

    lane = lax.broadcasted_iota(jnp.int32, (1, LANES), 1)
    normed = [acc / acc[:, V_HEAD_DIM:V_HEAD_DIM + 1] for acc in accs]
    second = pltpu.roll(normed[1], V_HEAD_DIM, axis=1)
    o_ref[...] = jnp.where(lane < V_HEAD_DIM, normed[0], second).astype(_BF16)


def _mixer_b_call(qb, kb, vb):
    b, s, _ = qb.shape
    pairs = N_HEADS_B // 2
    kv_spec = pl.BlockSpec((None, s, 2 * HEAD_PAD), lambda bi, g, i: (bi, 0, g))
    return pl.pallas_call(
        _mixer_b_kernel,
        grid=(b, pairs, s // B_Q_ROWS),
        in_specs=[pl.BlockSpec((None, B_Q_ROWS, 2 * HEAD_PAD), lambda bi, g, i: (bi, i, g)), kv_spec, kv_spec],
        out_specs=pl.BlockSpec((None, B_Q_ROWS, 2 * V_HEAD_DIM), lambda bi, g, i: (bi, i, g)),
        out_shape=jax.ShapeDtypeStruct((b, s, WIDTH_B), _BF16),
        scratch_shapes=[pltpu.VMEM((2, B_Q_ROWS, B_K_ROWS), _F32)] * 2
                       + [pltpu.VMEM((2, B_Q_ROWS, B_K_ROWS), _BF16)] * 2,
        name="mixer_b",
        compiler_params=_params(3),
    )(qb, kb, vb)


def _merge_kernel(x_ref, *refs):
    n_dil = len(DILATIONS)
    o_refs, lse_refs = refs[:n_dil], refs[n_dil:2 * n_dil]
    ob_ref, g_a_ref, g_b_ref, w_o_ref, x1_ref, o_stage, lse_stage = refs[2 * n_dil:]
    rows = x_ref.shape[0]
    for j, dil in enumerate(DILATIONS):
        for g in range(GROUPS_A):
            for r in range(dil):
                rows_r = pl.ds(r, rows // dil, stride=dil) if dil > 1 else pl.ds(0, rows)
                o_stage[j * GROUPS_A + g, rows_r, :] = o_refs[j][r, :, _group(g)].astype(_F32)
                lse_stage[j * GROUPS_A + g, rows_r, :] = lse_refs[j][r, :, _group(g)]
    merged = []
    for g in range(GROUPS_A):
        lses = [lse_stage[j * GROUPS_A + g] for j in range(n_dil)]
        top = lses[0]
        for l in lses[1:]:
            top = jnp.maximum(top, l)
        wts = [jnp.exp2(l - top) for l in lses]
        num = sum(w * o_stage[j * GROUPS_A + g] for j, w in enumerate(wts))
        merged.append(num / sum(wts))
    n_a = _rms(jnp.concatenate(merged, axis=-1), g_a_ref[...]).astype(_BF16)
    n_b = _rms(ob_ref[...].astype(_F32), g_b_ref[...]).astype(_BF16)
    x1_ref[...] = x_ref[...] + _dot(n_a, w_o_ref[:WIDTH_A, :]) + _dot(n_b, w_o_ref[WIDTH_A:, :])


def _merge_call(x, o_pats, lse_pats, o_b, g_a, g_b, w_o):
    b, s, _ = x.shape
    rows = PROJ_ROWS
    tok = lambda width: pl.BlockSpec((None, rows, width), lambda bi, i: (bi, i, 0))
    pat_specs = [_residue_spec(dil, rows) for dil in DILATIONS]
    stage = pltpu.VMEM((len(DILATIONS) * GROUPS_A, rows, LANES), _F32)
    return pl.pallas_call(
        _merge_kernel,
        grid=(b, s // rows),
        in_specs=[tok(D_MODEL)] + pat_specs * 2 + [tok(WIDTH_B)]
                 + [_const_spec(g_a.shape), _const_spec(g_b.shape), _const_spec(w_o.shape)],
        out_specs=tok(D_MODEL),
        out_shape=jax.ShapeDtypeStruct((b, s, D_MODEL), _F32),
        scratch_shapes=[stage, stage],
        name="merge",
        compiler_params=_params(2),
    )(x, *o_pats, *lse_pats, o_b, g_a, g_b, w_o)


def _ffn_kernel(x_ref, prev_ref, next_ref, g_ffn_ref, w_up_ref, w_conv_ref, b_conv_ref, w_down_ref, g_final_ref,
                y_ref, act_ref):
    i, last = pl.program_id(1), pl.num_programs(1) - 1
    rows = x_ref.shape[0]
    g = g_ffn_ref[...]
    x = x_ref[...]
    h_prev = jnp.where(i > 0, _rms(prev_ref[...], g), 0.0)
    h_next = jnp.where(i < last, _rms(next_ref[...], g), 0.0)
    hn = jnp.concatenate([h_prev, _rms(x, g), h_next], axis=0).astype(_BF16)
    ext = rows + 2 * SUBLANES

    def conv(lo):
        u = _dot(hn, w_up_ref[:, lo:lo + FFN_COLS])
        w = w_conv_ref[:, lo:lo + FFN_COLS]
        before = pltpu.roll(u, 1, axis=0)[SUBLANES:SUBLANES + rows]
        after = pltpu.roll(u, ext - 1, axis=0)[SUBLANES:SUBLANES + rows]
        return (before * w[0:1] + u[SUBLANES:SUBLANES + rows] * w[1:2] + after * w[2:3]
                + b_conv_ref[:, lo:lo + FFN_COLS])

    def activations(chunks):
        for c in chunks:
            gate, val = conv(c * FFN_COLS), conv(D_FF + c * FFN_COLS)
            act_ref[:, c * FFN_COLS:(c + 1) * FFN_COLS] = (gate * jax.nn.sigmoid(gate) * val).astype(_BF16)

    split = FFN_SPLIT * FFN_COLS
    activations(range(FFN_SPLIT))
    acc = _dot(act_ref[:, :split], w_down_ref[:split, :])
    activations(range(FFN_SPLIT, D_FF // FFN_COLS))
    acc = acc + _dot(act_ref[:, split:], w_down_ref[split:, :])
    y_ref[...] = _rms(x + acc, g_final_ref[...])


def _ffn_call(x1, g_ffn, w_up, w_conv, b_conv, w_down, g_final):
    b, s, _ = x1.shape
    rows = PROJ_ROWS
    blocks = rows // SUBLANES
    last_block = s // SUBLANES - 1
    tok = pl.BlockSpec((None, rows, D_MODEL), lambda bi, i: (bi, i, 0))
    prev = pl.BlockSpec((None, SUBLANES, D_MODEL), lambda bi, i: (bi, jnp.maximum(i * blocks - 1, 0), 0))
    nxt = pl.BlockSpec((None, SUBLANES, D_MODEL), lambda bi, i: (bi, jnp.minimum((i + 1) * blocks, last_block), 0))
    return pl.pallas_call(
        _ffn_kernel,
        grid=(b, s // rows),
        in_specs=[tok, prev, nxt, _const_spec(g_ffn.shape), _const_spec(w_up.shape), _const_spec(w_conv.shape),
                  _const_spec(b_conv.shape), _const_spec(w_down.shape), _const_spec(g_final.shape)],
        out_specs=tok,
        out_shape=jax.ShapeDtypeStruct((b, s, D_MODEL), _F32),
        scratch_shapes=[pltpu.VMEM((rows, D_FF), _BF16)],
        name="ffn",
        compiler_params=_params(2),
    )(x1, x1, x1, g_ffn, w_up, w_conv, b_conv, w_down, g_final)


def _rotate_half_cols(w, head_dim):
    k, n = w.shape
    w3 = w.reshape(k, n // head_dim, head_dim)
    half = head_dim // 2
    return jnp.concatenate([-w3[..., half:], w3[..., :half]], axis=-1).reshape(k, n)


def _pad_heads(w, n_heads, lo, width):
    k = w.shape[0]
    w3 = w.reshape(k, n_heads, -1)[..., lo:lo + width]
    return jnp.pad(w3, ((0, 0), (0, 0), (0, HEAD_PAD - width))).reshape(k, n_heads * HEAD_PAD)


def _prepare_weights(w_in, w_uq, w_ukv):
    wq, wk, wv = (w_in[:, j * WIDTH_A:(j + 1) * WIDTH_A] for j in range(3))
    lo = 3 * WIDTH_A
    w_cq = w_in[:, lo:lo + Q_LORA_RANK]
    w_ckv = w_in[:, lo + Q_LORA_RANK:lo + Q_LORA_RANK + KV_LORA_RANK]
    w_kr = w_in[:, lo + Q_LORA_RANK + KV_LORA_RANK:]
    place = lambda w: jnp.pad(w, ((0, 0), (QK_NOPE_DIM, HEAD_PAD - QK_NOPE_DIM - QK_ROPE_DIM)))
    w_ext = jnp.concatenate(
        [wq, wk, _rotate_half_cols(wq, HEAD_DIM_A), _rotate_half_cols(wk, HEAD_DIM_A), wv, w_cq, w_ckv,
         place(w_kr), place(_rotate_half_cols(w_kr, QK_ROPE_DIM))], axis=1).astype(_BF16)
    assert w_ext.shape[1] == _EXT_WIDTH

    per_q = QK_NOPE_DIM + QK_ROPE_DIM
    uq3 = w_uq.reshape(Q_LORA_RANK, N_HEADS_B, per_q)
    rot = _rotate_half_cols(uq3[..., QK_NOPE_DIM:].reshape(Q_LORA_RANK, -1), QK_ROPE_DIM)
    rot = jnp.pad(rot.reshape(Q_LORA_RANK, N_HEADS_B, QK_ROPE_DIM),
                  ((0, 0), (0, 0), (QK_NOPE_DIM, HEAD_PAD - per_q))).reshape(Q_LORA_RANK, WIDTH_B_PAD)
    w_uq_ext = jnp.concatenate([_pad_heads(w_uq, N_HEADS_B, 0, per_q), rot], axis=1).astype(_BF16)

    w_ukv_ext = jnp.concatenate([_pad_heads(w_ukv, N_HEADS_B, 0, QK_NOPE_DIM),
                                 _pad_heads(w_ukv, N_HEADS_B, QK_NOPE_DIM, V_HEAD_DIM)], axis=1).astype(_BF16)
    return w_ext, w_uq_ext, w_ukv_ext


def _rope_tables(s):
    pos = jnp.arange(s, dtype=_F32)[:, None]

    def cos_sin(dim):
        half = dim // 2
        inv_freq = ROPE_THETA ** (-jnp.arange(half, dtype=_F32) * (2.0 / dim))
        ang = pos * inv_freq[None, :]
        return jnp.tile(jnp.cos(ang), (1, 2)), jnp.tile(jnp.sin(ang), (1, 2))

    cos_a, sin_a = (jnp.tile(t, (1, LANES // HEAD_DIM_A)) for t in cos_sin(HEAD_DIM_A))
    scale_a = HEAD_DIM_A ** -0.5 * LOG2_E
    tab_a = jnp.concatenate([cos_a * scale_a, sin_a * scale_a, cos_a, sin_a], axis=1)

    cos_b, sin_b = cos_sin(QK_ROPE_DIM)
    tail = HEAD_PAD - QK_NOPE_DIM - QK_ROPE_DIM
    lay = lambda nope, t: jnp.concatenate([jnp.full((s, QK_NOPE_DIM), nope, _F32), t, jnp.zeros((s, tail), _F32)], 1)
    scale_b = (QK_NOPE_DIM + QK_ROPE_DIM) ** -0.5 * LOG2_E
    tab_b = jnp.concatenate([lay(1.0, cos_b) * scale_b, lay(0.0, sin_b) * scale_b, lay(0.0, cos_b), lay(0.0, sin_b)],
                            axis=1)
    return tab_a, tab_b


def _layer(x, tabs, g_attn, w_in, g_q_lora, w_uq, g_kv_lora, w_ukv, g_out_a, g_out_b, w_o):
    row = lambda g: g.reshape(1, -1)
    w_ext, w_uq_ext, w_ukv_ext = _prepare_weights(w_in, w_uq, w_ukv)
    a_ops, (qb, kb, vb) = _proj_call(x, *tabs, row(g_attn), w_ext, row(g_q_lora), w_uq_ext, row(g_kv_lora),
                                     w_ukv_ext)
    pats = [_mixer_a_call(*ops, dil) for ops, dil in zip(a_ops, DILATIONS)]
    o_b = _mixer_b_call(qb, kb, vb)
    return _merge_call(x, [p[0] for p in pats], [p[1] for p in pats], o_b, row(g_out_a), row(g_out_b),
                       w_o.astype(_BF16))


def _trunk(x, g_attn, w_in, g_q_lora, w_uq, g_kv_lora, w_ukv, g_out_a, g_out_b, w_o, g_ffn, w_up, w_conv, b_conv,
           w_down, g_final):
    depth = w_in.shape[0]
    assert depth == 1, "the ffn kernel applies the final norm, so it closes the trunk"
    assert all(window // 2 // dil == A_RADIUS for window, dil in DILATED_CONFIGS)
    tabs = _rope_tables(x.shape[1])
    x1 = _layer(x, tabs, g_attn[0], w_in[0], g_q_lora[0], w_uq[0], g_kv_lora[0], w_ukv[0], g_out_a[0], g_out_b[0],
                w_o[0])
    return _ffn_call(x1, g_ffn[0].reshape(1, -1), w_up[0].astype(_BF16), w_conv[0], b_conv[0].reshape(1, -1),
                     w_down[0].astype(_BF16), g_final.reshape(1, -1))


def kernel(x_prompt, x_sample, g_attn, w_in, g_q_lora, w_uq, g_kv_lora, w_ukv, g_out_a, g_out_b, w_o, g_ffn, w_up,
           w_conv, b_conv, w_down, g_final):
    weights = (g_attn, w_in, g_q_lora, w_uq, g_kv_lora, w_ukv, g_out_a, g_out_b, w_o, g_ffn, w_up, w_conv, b_conv,
               w_down, g_final)
    return (_trunk(x_prompt, *weights), _trunk(x_sample, *weights))
```

```python
import jax
import jax.numpy as jnp
from jax import lax
from jax.experimental import pallas as pl
from jax.experimental.pallas import tpu as pltpu

D_MODEL = 1024
N_HEADS_A = 8
HEAD_DIM_A = 64
DILATED_CONFIGS = ((128, 1), (512, 4), (2048, 16))
N_HEADS_B = 8
Q_LORA_RANK = 256
KV_LORA_RANK = 128
QK_NOPE_DIM = 64
QK_ROPE_DIM = 32
V_HEAD_DIM = 64
WIDTH_A = N_HEADS_A * HEAD_DIM_A
WIDTH_B = N_HEADS_B * V_HEAD_DIM
D_FF = 2816
ROPE_THETA = 10000.0
EPS = 1e-6
NEG = -1e30
LOG2_E = 1.4426950408889634

LANES = 128
SUBLANES = 8
HEAD_PAD = LANES
WIDTH_B_PAD = N_HEADS_B * HEAD_PAD
GROUPS_A = WIDTH_A // LANES
VMEM_LIMIT_BYTES = 56 * 1024 * 1024

DILATIONS = tuple(dil for _, dil in DILATED_CONFIGS)
PROJ_ROWS = 512
A_ROWS = 128
A_UNROLL = 8
A_RADIUS = 64
A_WINDOW = A_ROWS + 2 * A_RADIUS
B_Q_ROWS = 512
B_K_ROWS = 512
FFN_COLS = 256
FFN_SPLIT = 6

_OFF_Q, _OFF_K, _OFF_V = 0, WIDTH_A, 2 * WIDTH_A
_OFF_CQ = 3 * WIDTH_A
_OFF_CKV = _OFF_CQ + Q_LORA_RANK
_EXT_WIDTH = _OFF_CKV + KV_LORA_RANK + HEAD_PAD

_BF16 = jnp.bfloat16
_F32 = jnp.float32


def _rms(xf, g):
    return xf * lax.rsqrt(jnp.mean(xf * xf, axis=-1, keepdims=True) + EPS) * g


def _dot(a, b):
    return jnp.dot(a, b, preferred_element_type=_F32)


def _dot_nt(a, b):
    return lax.dot_general(a, b, (((1,), (1,)), ((), ())), preferred_element_type=_F32)


def _rope(t, cos, sin_signed, half, first_half):
    partner = jnp.where(first_half, pltpu.roll(t, LANES - half, axis=1), pltpu.roll(t, half, axis=1))
    return t * cos + partner * sin_signed


def _group(g):
    return slice(g * LANES, (g + 1) * LANES)


def _const_spec(shape):
    return pl.BlockSpec(shape, lambda *_: (0,) * len(shape))


def _params(n_axes):
    return pltpu.CompilerParams(dimension_semantics=("parallel",) * n_axes, vmem_limit_bytes=VMEM_LIMIT_BYTES)


def _proj_kernel(x_ref, tab_a_ref, tab_b_ref, g_attn_ref, w_ext_ref, g_q_ref, w_uq_ref, g_kv_ref, w_ukv_ref, *refs):
    n_dil = len(DILATIONS)
    a_refs = [refs[3 * j:3 * j + 3] for j in range(n_dil)]
    qb_ref, kb_ref, vb_ref = refs[3 * n_dil:3 * n_dil + 3]
    stage_refs = refs[3 * n_dil + 3:]
    rows = x_ref.shape[0]
    hn = _rms(x_ref[...], g_attn_ref[...]).astype(_BF16)

    def proj(lo, width):
        return _dot(hn, w_ext_ref[:, lo:lo + width])

    def tab(ref, j):
        return ref[:, _group(j)]

    lane = lax.broadcasted_iota(jnp.int32, (1, LANES), 1)
    half_a, half_b = HEAD_DIM_A // 2, QK_ROPE_DIM // 2
    first_a, first_b = (lane % HEAD_DIM_A) < half_a, (lane % QK_ROPE_DIM) < half_b

    cqn = _rms(proj(_OFF_CQ, Q_LORA_RANK), g_q_ref[...]).astype(_BF16)
    qb = _dot(cqn, w_uq_ref[...])
    for h in range(N_HEADS_B):
        qb_ref[:, _group(h)] = _rope(qb[:, _group(h)], tab(tab_b_ref, 0), tab(tab_b_ref, 1), half_b,
                                     first_b).astype(_BF16)

    rest = proj(_OFF_CKV, KV_LORA_RANK + HEAD_PAD)
    ckvn = _rms(rest[:, :KV_LORA_RANK], g_kv_ref[...]).astype(_BF16)
    kv = _dot(ckvn, w_ukv_ref[...])
    kb_rot = _rope(rest[:, KV_LORA_RANK:], tab(tab_b_ref, 2), tab(tab_b_ref, 3), half_b, first_b)
    one_col = (lane == V_HEAD_DIM).astype(_F32)
    for h in range(N_HEADS_B):
        kb_ref[:, _group(h)] = (kv[:, _group(h)] + kb_rot).astype(_BF16)
        vb_ref[:, _group(h)] = (kv[:, WIDTH_B_PAD + h * HEAD_PAD:WIDTH_B_PAD + (h + 1) * HEAD_PAD]
                                + one_col).astype(_BF16)

    for j, off in enumerate((_OFF_Q, _OFF_K, _OFF_V)):
        stage_ref = stage_refs[j]
        main = proj(off, WIDTH_A)
        for g in range(GROUPS_A):
            if off == _OFF_V:
                stage_ref[g] = main[:, _group(g)]
            else:
                stage_ref[g] = _rope(main[:, _group(g)], tab(tab_a_ref, 2 * j), tab(tab_a_ref, 2 * j + 1), half_a,
                                     first_a)
        for outs, dil in zip(a_refs, DILATIONS):
            for g in range(GROUPS_A):
                for r in range(dil):
                    rows_r = pl.ds(r, rows // dil, stride=dil) if dil > 1 else pl.ds(0, rows)
                    outs[j][r, :, _group(g)] = stage_ref[g, rows_r, :].astype(_BF16)


def _residue_spec(dil, rows):
    return pl.BlockSpec((None, dil, rows // dil, WIDTH_A), lambda bi, i: (bi, 0, i, 0))


def _proj_call(x, tab_a, tab_b, g_attn, w_ext, g_q, w_uq, g_kv, w_ukv):
    b, s, _ = x.shape
    rows = PROJ_ROWS
    tok = lambda width: pl.BlockSpec((None, rows, width), lambda bi, i: (bi, i, 0))
    tabs = lambda width: pl.BlockSpec((rows, width), lambda bi, i: (i, 0))
    out_b = jax.ShapeDtypeStruct((b, s, WIDTH_B_PAD), _BF16)
    a_specs = [_residue_spec(dil, rows) for dil in DILATIONS for _ in range(3)]
    a_shapes = [jax.ShapeDtypeStruct((b, dil, s // dil, WIDTH_A), _BF16) for dil in DILATIONS for _ in range(3)]
    outs = pl.pallas_call(
        _proj_kernel,
        grid=(b, s // rows),
        in_specs=[tok(D_MODEL), tabs(4 * LANES), tabs(4 * LANES),
                  _const_spec(g_attn.shape), _const_spec(w_ext.shape), _const_spec(g_q.shape),
                  _const_spec(w_uq.shape), _const_spec(g_kv.shape), _const_spec(w_ukv.shape)],
        out_specs=a_specs + [tok(WIDTH_B_PAD)] * 3,
        out_shape=a_shapes + [out_b] * 3,
        scratch_shapes=[pltpu.VMEM((GROUPS_A, rows, LANES), _F32)] * 3,
        name="proj",
        compiler_params=_params(2),
    )(x, tab_a, tab_b, g_attn, w_ext, g_q, w_uq, g_kv, w_ukv)
    n = 3 * len(DILATIONS)
    return [outs[3 * j:3 * j + 3] for j in range(len(DILATIONS))], outs[n:]


def _mixer_a_kernel(q_ref, k_ref, v_ref, o_ref, lse_ref, bias_ref):
    residues, length, _ = q_ref.shape
    tiles = length // A_ROWS
    lane = lax.broadcasted_iota(jnp.int32, (1, LANES), 1)
    first_head = lane < HEAD_DIM_A
    col_minus_row = (lax.broadcasted_iota(jnp.int32, (A_ROWS, A_WINDOW), 1)
                     - lax.broadcasted_iota(jnp.int32, (A_ROWS, A_WINDOW), 0))
    for case in range(3):
        bias_ref[case] = jnp.where(jnp.abs(col_minus_row - case * A_RADIUS) <= A_RADIUS, 0.0, NEG)

    def tile(t, carry):
        r = t // tiles
        a0 = pl.multiple_of((t % tiles) * A_ROWS, A_ROWS)
        ws = pl.multiple_of(jnp.clip(a0 - A_RADIUS, 0, length - A_WINDOW), A_RADIUS)
        bias = bias_ref[(a0 - ws) // A_RADIUS]
        q = q_ref[r, pl.ds(a0, A_ROWS), :]
        kw = k_ref[r, pl.ds(ws, A_WINDOW), :]
        vw = v_ref[r, pl.ds(ws, A_WINDOW), :]
        outs, lses = [], []
        for head_sel in (first_head, jnp.logical_not(first_head)):
            s = _dot_nt(jnp.where(head_sel, q, jnp.zeros_like(q)), kw) + bias
            m = jnp.max(s, axis=-1, keepdims=True)
            p = jnp.exp2(s - m)
            den = jnp.sum(p, axis=-1, keepdims=True)
            outs.append(_dot(p.astype(_BF16), vw) / den)
            lses.append(m + jnp.log2(den))
        o_ref[r, pl.ds(a0, A_ROWS), :] = jnp.where(first_head, outs[0], outs[1]).astype(_BF16)
        lse_ref[r, pl.ds(a0, A_ROWS), :] = jnp.where(first_head, lses[0], lses[1])
        return carry

    lax.fori_loop(0, residues * tiles, tile, 0, unroll=A_UNROLL)


def _mixer_a_call(qa, ka, va):
    b, dil, length, _ = qa.shape
    assert length % A_ROWS == 0 and length >= A_WINDOW
    spec = pl.BlockSpec((None, dil, length, LANES), lambda bi, g: (bi, 0, 0, g))
    return pl.pallas_call(
        _mixer_a_kernel,
        grid=(b, GROUPS_A),
        in_specs=[spec] * 3,
        out_specs=[spec] * 2,
        out_shape=[jax.ShapeDtypeStruct(qa.shape, _BF16), jax.ShapeDtypeStruct(qa.shape, _F32)],
        scratch_shapes=[pltpu.VMEM((3, A_ROWS, A_WINDOW), _F32)],
        name=f"mixer_a_dil{dil}",
        compiler_params=_params(2),
    )(qa, ka, va)


def _mixer_b_kernel(q_ref, k_ref, v_ref, o_ref, s_even, s_odd, p_even, p_odd):
    rows = q_ref.shape[0]
    chunks = k_ref.shape[0] // B_K_ROWS
    assert chunks % 2 == 0 and chunks >= 4
    heads = [_group(hh) for hh in range(2)]

    def rows_of(c):
        return pl.ds(pl.multiple_of(c * B_K_ROWS, B_K_ROWS), B_K_ROWS)

    def scores(c, s_ref):
        for hh, sl in enumerate(heads):
            s_ref[hh] = _dot_nt(q_ref[:, sl], k_ref[rows_of(c), sl])

    def softmax(s_ref, p_ref, maxes):
        out = []
        for hh, m in enumerate(maxes):
            s = s_ref[hh]
            m_new = jnp.maximum(m, jnp.max(s, axis=-1, keepdims=True))
            p_ref[hh] = jnp.exp2(s - m_new).astype(_BF16)
            out.append((m_new, jnp.exp2(m - m_new)))
        return [o[0] for o in out], [o[1] for o in out]

    def values(c, p_ref, alphas, accs):
        return [alpha * acc + _dot(p_ref[hh], v_ref[rows_of(c), sl])
                for hh, (sl, alpha, acc) in enumerate(zip(heads, alphas, accs))]

    maxes = [jnp.full((rows, 1), NEG, _F32)] * 2
    accs = [jnp.zeros((rows, HEAD_PAD), _F32)] * 2
    scores(0, s_even)
    scores(1, s_odd)
    maxes, alphas = softmax(s_even, p_even, maxes)

    def pair(j, carry):
        maxes, alphas, accs = carry
        c = 2 * j + 1
        scores(c + 1, s_even)
        maxes, alphas_odd = softmax(s_odd, p_odd, maxes)
        accs = values(c - 1, p_even, alphas, accs)
        scores(c + 2, s_odd)
        maxes, alphas = softmax(s_even, p_even, maxes)
        accs = values(c, p_odd, alphas_odd, accs)
        return maxes, alphas, accs

    maxes, alphas, accs = lax.fori_loop(0, chunks // 2 - 1, pair, (maxes, alphas, accs), unroll=True)
    maxes, alphas_odd = softmax(s_odd, p_odd, maxes)
    accs = values(chunks - 2, p_even, alphas, accs)
    accs = values(chunks - 1, p_odd, alphas_odd, accs)

    lane = lax.broadcasted_iota(jnp.int32, (1, LANES), 1)
    normed = [acc / acc[:, V_HEAD_DIM:V_HEAD_DIM + 1] for acc in accs]
    second = pltpu.roll(normed[1], V_HEAD_DIM, axis=1)
    o_ref[...] = jnp.where(lane < V_HEAD_DIM, normed[0], second).astype(_BF16)


def _mixer_b_call(qb, kb, vb):
    b, s, _ = qb.shape
    pairs = N_HEADS_B // 2
    kv_spec = pl.BlockSpec((None, s, 2 * HEAD_PAD), lambda bi, g, i: (bi, 0, g))
    return pl.pallas_call(
        _mixer_b_kernel,
        grid=(b, pairs, s // B_Q_ROWS),
        in_specs=[pl.BlockSpec((None, B_Q_ROWS, 2 * HEAD_PAD), lambda bi, g, i: (bi, i, g)), kv_spec, kv_spec],
        out_specs=pl.BlockSpec((None, B_Q_ROWS, 2 * V_HEAD_DIM), lambda bi, g, i: (bi, i, g)),
        out_shape=jax.ShapeDtypeStruct((b, s, WIDTH_B), _BF16),
        scratch_shapes=[pltpu.VMEM((2, B_Q_ROWS, B_K_ROWS), _F32)] * 2
                       + [pltpu.VMEM((2, B_Q_ROWS, B_K_ROWS), _BF16)] * 2,
        name="mixer_b",
        compiler_params=_params(3),
    )(qb, kb, vb)


def _merge_kernel(x_ref, *refs):
    n_dil = len(DILATIONS)
    o_refs, lse_refs = refs[:n_dil], refs[n_dil:2 * n_dil]
    ob_ref, g_a_ref, g_b_ref, w_o_ref, x1_ref, o_stage, lse_stage = refs[2 * n_dil:]
    rows = x_ref.shape[0]
    for j, dil in enumerate(DILATIONS):
        for g in range(GROUPS_A):
            for r in range(dil):
                rows_r = pl.ds(r, rows // dil, stride=dil) if dil > 1 else pl.ds(0, rows)
                o_stage[j * GROUPS_A + g, rows_r, :] = o_refs[j][r, :, _group(g)].astype(_F32)
                lse_stage[j * GROUPS_A + g, rows_r, :] = lse_refs[j][r, :, _group(g)]
    merged = []
    for g in range(GROUPS_A):
        lses = [lse_stage[j * GROUPS_A + g] for j in range(n_dil)]
        top = lses[0]
        for l in lses[1:]:
            top = jnp.maximum(top, l)
        wts = [jnp.exp2(l - top) for l in lses]
        num = sum(w * o_stage[j * GROUPS_A + g] for j, w in enumerate(wts))
        merged.append(num / sum(wts))
    n_a = _rms(jnp.concatenate(merged, axis=-1), g_a_ref[...]).astype(_BF16)
    n_b = _rms(ob_ref[...].astype(_F32), g_b_ref[...]).astype(_BF16)
    x1_ref[...] = x_ref[...] + _dot(n_a, w_o_ref[:WIDTH_A, :]) + _dot(n_b, w_o_ref[WIDTH_A:, :])


def _merge_call(x, o_pats, lse_pats, o_b, g_a, g_b, w_o):
    b, s, _ = x.shape
    rows = PROJ_ROWS
    tok = lambda width: pl.BlockSpec((None, rows, width), lambda bi, i: (bi, i, 0))
    pat_specs = [_residue_spec(dil, rows) for dil in DILATIONS]
    stage = pltpu.VMEM((len(DILATIONS) * GROUPS_A, rows, LANES), _F32)
    return pl.pallas_call(
        _merge_kernel,
        grid=(b, s // rows),
        in_specs=[tok(D_MODEL)] + pat_specs * 2 + [tok(WIDTH_B)]
                 + [_const_spec(g_a.shape), _const_spec(g_b.shape), _const_spec(w_o.shape)],
        out_specs=tok(D_MODEL),
        out_shape=jax.ShapeDtypeStruct((b, s, D_MODEL), _F32),
        scratch_shapes=[stage, stage],
        name="merge",
        compiler_params=_params(2),
    )(x, *o_pats, *lse_pats, o_b, g_a, g_b, w_o)


def _ffn_kernel(x_ref, prev_ref, next_ref, g_ffn_ref, w_up_ref, w_conv_ref, b_conv_ref, w_down_ref, g_final_ref,
                y_ref, act_ref):
    i, last = pl.program_id(1), pl.num_programs(1) - 1
    rows = x_ref.shape[0]
    g = g_ffn_ref[...]
    x = x_ref[...]
    h_prev = jnp.where(i > 0, _rms(prev_ref[...], g), 0.0)
    h_next = jnp.where(i < last, _rms(next_ref[...], g), 0.0)
    hn = jnp.concatenate([h_prev, _rms(x, g), h_next], axis=0).astype(_BF16)
    ext = rows + 2 * SUBLANES

    def conv(lo):
        u = _dot(hn, w_up_ref[:, lo:lo + FFN_COLS])
        w = w_conv_ref[:, lo:lo + FFN_COLS]
        before = pltpu.roll(u, 1, axis=0)[SUBLANES:SUBLANES + rows]
        after = pltpu.roll(u, ext - 1, axis=0)[SUBLANES:SUBLANES + rows]
        return (before * w[0:1] + u[SUBLANES:SUBLANES + rows] * w[1:2] + after * w[2:3]
                + b_conv_ref[:, lo:lo + FFN_COLS])

    def activations(chunks):
        for c in chunks:
            gate, val = conv(c * FFN_COLS), conv(D_FF + c * FFN_COLS)
            act_ref[:, c * FFN_COLS:(c + 1) * FFN_COLS] = (gate * jax.nn.sigmoid(gate) * val).astype(_BF16)

    split = FFN_SPLIT * FFN_COLS
    activations(range(FFN_SPLIT))
    acc = _dot(act_ref[:, :split], w_down_ref[:split, :])
    activations(range(FFN_SPLIT, D_FF // FFN_COLS))
    acc = acc + _dot(act_ref[:, split:], w_down_ref[split:, :])
    y_ref[...] = _rms(x + acc, g_final_ref[...])


def _ffn_call(x1, g_ffn, w_up, w_conv, b_conv, w_down, g_final):
    b, s, _ = x1.shape
    rows = PROJ_ROWS
    blocks = rows // SUBLANES
    last_block = s // SUBLANES - 1
    tok = pl.BlockSpec((None, rows, D_MODEL), lambda bi, i: (bi, i, 0))
    prev = pl.BlockSpec((None, SUBLANES, D_MODEL), lambda bi, i: (bi, jnp.maximum(i * blocks - 1, 0), 0))
    nxt = pl.BlockSpec((None, SUBLANES, D_MODEL), lambda bi, i: (bi, jnp.minimum((i + 1) * blocks, last_block), 0))
    return pl.pallas_call(
        _ffn_kernel,
        grid=(b, s // rows),
        in_specs=[tok, prev, nxt, _const_spec(g_ffn.shape), _const_spec(w_up.shape), _const_spec(w_conv.shape),
                  _const_spec(b_conv.shape), _const_spec(w_down.shape), _const_spec(g_final.shape)],
        out_specs=tok,
        out_shape=jax.ShapeDtypeStruct((b, s, D_MODEL), _F32),
        scratch_shapes=[pltpu.VMEM((rows, D_FF), _BF16)],
        name="ffn",
        compiler_params=_params(2),
    )(x1, x1, x1, g_ffn, w_up, w_conv, b_conv, w_down, g_final)


def _pad_heads(w, n_heads, lo, width):
    k = w.shape[0]
    w3 = w.reshape(k, n_heads, -1)[..., lo:lo + width]
    return jnp.pad(w3, ((0, 0), (0, 0), (0, HEAD_PAD - width))).reshape(k, n_heads * HEAD_PAD)


def _prepare_weights(w_in, w_uq, w_ukv):
    wq, wk, wv = (w_in[:, j * WIDTH_A:(j + 1) * WIDTH_A] for j in range(3))
    lo = 3 * WIDTH_A
    w_cq = w_in[:, lo:lo + Q_LORA_RANK]
    w_ckv = w_in[:, lo + Q_LORA_RANK:lo + Q_LORA_RANK + KV_LORA_RANK]
    w_kr = w_in[:, lo + Q_LORA_RANK + KV_LORA_RANK:]
    place = lambda w: jnp.pad(w, ((0, 0), (QK_NOPE_DIM, HEAD_PAD - QK_NOPE_DIM - QK_ROPE_DIM)))
    w_ext = jnp.concatenate([wq, wk, wv, w_cq, w_ckv, place(w_kr)], axis=1).astype(_BF16)
    assert w_ext.shape[1] == _EXT_WIDTH
    w_uq_ext = _pad_heads(w_uq, N_HEADS_B, 0, QK_NOPE_DIM + QK_ROPE_DIM).astype(_BF16)

    w_ukv_ext = jnp.concatenate([_pad_heads(w_ukv, N_HEADS_B, 0, QK_NOPE_DIM),
                                 _pad_heads(w_ukv, N_HEADS_B, QK_NOPE_DIM, V_HEAD_DIM)], axis=1).astype(_BF16)
    return w_ext, w_uq_ext, w_ukv_ext


def _rope_tables(s):
    pos = jnp.arange(s, dtype=_F32)[:, None]

    def cos_sin(dim):
        half = dim // 2
        inv_freq = ROPE_THETA ** (-jnp.arange(half, dtype=_F32) * (2.0 / dim))
        ang = pos * inv_freq[None, :]
        return jnp.tile(jnp.cos(ang), (1, 2)), jnp.concatenate([-jnp.sin(ang), jnp.sin(ang)], axis=1)

    cos_a, sin_a = (jnp.tile(t, (1, LANES // HEAD_DIM_A)) for t in cos_sin(HEAD_DIM_A))
    scale_a = HEAD_DIM_A ** -0.5 * LOG2_E
    tab_a = jnp.concatenate([cos_a * scale_a, sin_a * scale_a, cos_a, sin_a], axis=1)

    cos_b, sin_b = cos_sin(QK_ROPE_DIM)
    tail = HEAD_PAD - QK_NOPE_DIM - QK_ROPE_DIM
    lay = lambda nope, t: jnp.concatenate([jnp.full((s, QK_NOPE_DIM), nope, _F32), t, jnp.zeros((s, tail), _F32)], 1)
    scale_b = (QK_NOPE_DIM + QK_ROPE_DIM) ** -0.5 * LOG2_E
    tab_b = jnp.concatenate([lay(1.0, cos_b) * scale_b, lay(0.0, sin_b) * scale_b, lay(0.0, cos_b), lay(0.0, sin_b)],
                            axis=1)
    return tab_a, tab_b


def _layer(x, tabs, g_attn, w_in, g_q_lora, w_uq, g_kv_lora, w_ukv, g_out_a, g_out_b, w_o):
    row = lambda g: g.reshape(1, -1)
    w_ext, w_uq_ext, w_ukv_ext = _prepare_weights(w_in, w_uq, w_ukv)
    a_ops, (qb, kb, vb) = _proj_call(x, *tabs, row(g_attn), w_ext, row(g_q_lora), w_uq_ext, row(g_kv_lora),
                                     w_ukv_ext)
    pats = [_mixer_a_call(*ops) for ops in a_ops]
    o_b = _mixer_b_call(qb, kb, vb)
    return _merge_call(x, [p[0] for p in pats], [p[1] for p in pats], o_b, row(g_out_a), row(g_out_b),
                       w_o.astype(_BF16))


def _trunk(x, g_attn, w_in, g_q_lora, w_uq, g_kv_lora, w_ukv, g_out_a, g_out_b, w_o, g_ffn, w_up, w_conv, b_conv,
           w_down, g_final):
    depth = w_in.shape[0]
    assert depth == 1, "the ffn kernel applies the final norm, so it closes the trunk"
    assert all(window // 2 // dil == A_RADIUS for window, dil in DILATED_CONFIGS)
    tabs = _rope_tables(x.shape[1])
    x1 = _layer(x, tabs, g_attn[0], w_in[0], g_q_lora[0], w_uq[0], g_kv_lora[0], w_ukv[0], g_out_a[0], g_out_b[0],
                w_o[0])
    return _ffn_call(x1, g_ffn[0].reshape(1, -1), w_up[0].astype(_BF16), w_conv[0], b_conv[0].reshape(1, -1),
                     w_down[0].astype(_BF16), g_final.reshape(1, -1))


def kernel(x_prompt, x_sample, g_attn, w_in, g_q_lora, w_uq, g_kv_lora, w_ukv, g_out_a, g_out_b, w_o, g_ffn, w_up,
           w_conv, b_conv, w_down, g_final):
    weights = (g_attn, w_in, g_q_lora, w_uq, g_kv_lora, w_ukv, g_out_a, g_out_b, w_o, g_ffn, w_up, w_conv, b_conv,
               w_down, g_final)
    return (_trunk(x_prompt, *weights), _trunk(x_sample, *weights))
```

```python
import jax
import jax.numpy as jnp
from jax import lax
from jax.experimental import pallas as pl
from jax.experimental.pallas import tpu as pltpu

D_MODEL = 1024
N_HEADS_A = 8
HEAD_DIM_A = 64
DILATED_CONFIGS = ((128, 1), (512, 4), (2048, 16))
N_HEADS_B = 8
Q_LORA_RANK = 256
KV_LORA_RANK = 128
QK_NOPE_DIM = 64
QK_ROPE_DIM = 32
V_HEAD_DIM = 64
WIDTH_A = N_HEADS_A * HEAD_DIM_A
WIDTH_B = N_HEADS_B * V_HEAD_DIM
D_FF = 2816
ROPE_THETA = 10000.0
EPS = 1e-6
NEG = -1e30
LOG2_E = 1.4426950408889634

LANES = 128
SUBLANES = 8
HEAD_PAD = LANES
WIDTH_B_PAD = N_HEADS_B * HEAD_PAD
GROUPS_A = WIDTH_A // LANES
VMEM_LIMIT_BYTES = 56 * 1024 * 1024

DILATIONS = tuple(dil for _, dil in DILATED_CONFIGS)
PROJ_ROWS = 512
FFN_ROWS = 1024
A_ROWS = 128
A_RADIUS = 64
A_WINDOW = A_ROWS + 2 * A_RADIUS
B_Q_ROWS = 512
B_K_ROWS = 512
FFN_COLS = 256
FFN_SPLIT = 6

_OFF_Q, _OFF_K, _OFF_V = 0, WIDTH_A, 2 * WIDTH_A
_OFF_CQ = 3 * WIDTH_A
_OFF_CKV = _OFF_CQ + Q_LORA_RANK
_EXT_WIDTH = _OFF_CKV + KV_LORA_RANK + HEAD_PAD

_BF16 = jnp.bfloat16
_F32 = jnp.float32


def _rms(xf, g):
    return xf * lax.rsqrt(jnp.mean(xf * xf, axis=-1, keepdims=True) + EPS) * g


def _dot(a, b):
    return jnp.dot(a, b, preferred_element_type=_F32)


def _dot_nt(a, b):
    return lax.dot_general(a, b, (((1,), (1,)), ((), ())), preferred_element_type=_F32)


def _rope(t, cos, sin_signed, half, first_half):
    partner = jnp.where(first_half, pltpu.roll(t, LANES - half, axis=1), pltpu.roll(t, half, axis=1))
    return t * cos + partner * sin_signed


def _group(g):
    return slice(g * LANES, (g + 1) * LANES)


def _const_spec(shape):
    return pl.BlockSpec(shape, lambda *_: (0,) * len(shape))


def _params(n_axes):
    return pltpu.CompilerParams(dimension_semantics=("parallel",) * n_axes, vmem_limit_bytes=VMEM_LIMIT_BYTES)


def _proj_kernel(x_ref, tab_a_ref, tab_b_ref, g_attn_ref, w_ext_ref, g_q_ref, w_uq_ref, g_kv_ref, w_ukv_ref, *refs):
    n_dil = len(DILATIONS)
    a_refs = [refs[3 * j:3 * j + 3] for j in range(n_dil)]
    qb_ref, kb_ref, vb_ref = refs[3 * n_dil:3 * n_dil + 3]
    stage_refs = refs[3 * n_dil + 3:]
    rows = x_ref.shape[0]
    hn = _rms(x_ref[...], g_attn_ref[...]).astype(_BF16)

    def proj(lo, width):
        return _dot(hn, w_ext_ref[:, lo:lo + width])

    def tab(ref, j):
        return ref[:, _group(j)]

    lane = lax.broadcasted_iota(jnp.int32, (1, LANES), 1)
    half_a, half_b = HEAD_DIM_A // 2, QK_ROPE_DIM // 2
    first_a, first_b = (lane % HEAD_DIM_A) < half_a, (lane % QK_ROPE_DIM) < half_b

    cqn = _rms(proj(_OFF_CQ, Q_LORA_RANK), g_q_ref[...]).astype(_BF16)
    qb = _dot(cqn, w_uq_ref[...])
    for h in range(N_HEADS_B):
        qb_ref[:, _group(h)] = _rope(qb[:, _group(h)], tab(tab_b_ref, 0), tab(tab_b_ref, 1), half_b,
                                     first_b).astype(_BF16)

    rest = proj(_OFF_CKV, KV_LORA_RANK + HEAD_PAD)
    ckvn = _rms(rest[:, :KV_LORA_RANK], g_kv_ref[...]).astype(_BF16)
    kv = _dot(ckvn, w_ukv_ref[...])
    kb_rot = _rope(rest[:, KV_LORA_RANK:], tab(tab_b_ref, 2), tab(tab_b_ref, 3), half_b, first_b)
    one_col = (lane == V_HEAD_DIM).astype(_F32)
    for h in range(N_HEADS_B):
        kb_ref[:, _group(h)] = (kv[:, _group(h)] + kb_rot).astype(_BF16)
        vb_ref[:, _group(h)] = (kv[:, WIDTH_B_PAD + h * HEAD_PAD:WIDTH_B_PAD + (h + 1) * HEAD_PAD]
                                + one_col).astype(_BF16)

    for j, off in enumerate((_OFF_Q, _OFF_K, _OFF_V)):
        stage_ref = stage_refs[j]
        main = proj(off, WIDTH_A)
        for g in range(GROUPS_A):
            if off == _OFF_V:
                stage_ref[g] = main[:, _group(g)]
            else:
                stage_ref[g] = _rope(main[:, _group(g)], tab(tab_a_ref, 2 * j), tab(tab_a_ref, 2 * j + 1), half_a,
                                     first_a)
        for outs, dil in zip(a_refs, DILATIONS):
            for g in range(GROUPS_A):
                for r in range(dil):
                    rows_r = pl.ds(r, rows // dil, stride=dil) if dil > 1 else pl.ds(0, rows)
                    outs[j][r, :, _group(g)] = stage_ref[g, rows_r, :].astype(_BF16)


def _residue_spec(dil, rows):
    return pl.BlockSpec((None, dil, rows // dil, WIDTH_A), lambda bi, i: (bi, 0, i, 0))


def _proj_call(x, tab_a, tab_b, g_attn, w_ext, g_q, w_uq, g_kv, w_ukv):
    b, s, _ = x.shape
    rows = PROJ_ROWS
    tok = lambda width: pl.BlockSpec((None, rows, width), lambda bi, i: (bi, i, 0))
    tabs = lambda width: pl.BlockSpec((rows, width), lambda bi, i: (i, 0))
    out_b = jax.ShapeDtypeStruct((b, s, WIDTH_B_PAD), _BF16)
    a_specs = [_residue_spec(dil, rows) for dil in DILATIONS for _ in range(3)]
    a_shapes = [jax.ShapeDtypeStruct((b, dil, s // dil, WIDTH_A), _BF16) for dil in DILATIONS for _ in range(3)]
    outs = pl.pallas_call(
        _proj_kernel,
        grid=(b, s // rows),
        in_specs=[tok(D_MODEL), tabs(4 * LANES), tabs(4 * LANES),
                  _const_spec(g_attn.shape), _const_spec(w_ext.shape), _const_spec(g_q.shape),
                  _const_spec(w_uq.shape), _const_spec(g_kv.shape), _const_spec(w_ukv.shape)],
        out_specs=a_specs + [tok(WIDTH_B_PAD)] * 3,
        out_shape=a_shapes + [out_b] * 3,
        scratch_shapes=[pltpu.VMEM((GROUPS_A, rows, LANES), _F32)] * 3,
        name="proj",
        compiler_params=_params(2),
    )(x, tab_a, tab_b, g_attn, w_ext, g_q, w_uq, g_kv, w_ukv)
    n = 3 * len(DILATIONS)
    return [outs[3 * j:3 * j + 3] for j in range(len(DILATIONS))], outs[n:]


def _mixer_a_kernel(q_ref, k_ref, v_ref, o_ref, lse_ref, bias_ref):
    residues, length, _ = q_ref.shape
    tiles = length // A_ROWS
    lane = lax.broadcasted_iota(jnp.int32, (1, LANES), 1)
    first_head = lane < HEAD_DIM_A
    col_minus_row = (lax.broadcasted_iota(jnp.int32, (A_ROWS, A_WINDOW), 1)
                     - lax.broadcasted_iota(jnp.int32, (A_ROWS, A_WINDOW), 0))
    for case in range(3):
        bias_ref[case] = jnp.where(jnp.abs(col_minus_row - case * A_RADIUS) <= A_RADIUS, 0.0, NEG)

    def tile(t, carry):
        r = t // tiles
        a0 = pl.multiple_of((t % tiles) * A_ROWS, A_ROWS)
        ws = pl.multiple_of(jnp.clip(a0 - A_RADIUS, 0, length - A_WINDOW), A_RADIUS)
        bias = bias_ref[(a0 - ws) // A_RADIUS]
        q = q_ref[r, pl.ds(a0, A_ROWS), :]
        kw = k_ref[r, pl.ds(ws, A_WINDOW), :]
        vw = v_ref[r, pl.ds(ws, A_WINDOW), :]
        nums, maxes, dens = [], [], []
        for head_sel in (first_head, jnp.logical_not(first_head)):
            s = _dot_nt(jnp.where(head_sel, q, jnp.zeros_like(q)), kw) + bias
            m = jnp.max(s, axis=-1, keepdims=True)
            p = jnp.exp2(s - m)
            nums.append(_dot(p.astype(_BF16), vw))
            maxes.append(m)
            dens.append(jnp.sum(p, axis=-1, keepdims=True))
        num, m, den = (jnp.where(first_head, a, b) for a, b in (nums, maxes, dens))
        o_ref[r, pl.ds(a0, A_ROWS), :] = (num / den).astype(_BF16)
        lse_ref[r, pl.ds(a0, A_ROWS), :] = m + jnp.log2(den)
        return carry

    lax.fori_loop(0, residues * tiles, tile, 0, unroll=True)


def _mixer_a_call(qa, ka, va):
    b, dil, length, _ = qa.shape
    assert length % A_ROWS == 0 and length >= A_WINDOW
    spec = pl.BlockSpec((None, dil, length, LANES), lambda bi, g: (bi, 0, 0, g))
    return pl.pallas_call(
        _mixer_a_kernel,
        grid=(b, GROUPS_A),
        in_specs=[spec] * 3,
        out_specs=[spec] * 2,
        out_shape=[jax.ShapeDtypeStruct(qa.shape, _BF16), jax.ShapeDtypeStruct(qa.shape, _F32)],
        scratch_shapes=[pltpu.VMEM((3, A_ROWS, A_WINDOW), _F32)],
        name=f"mixer_a_dil{dil}",
        compiler_params=_params(2),
    )(qa, ka, va)


def _mixer_b_kernel(q_ref, k_ref, v_ref, o_ref, s_even, s_odd, p_even, p_odd):
    rows = q_ref.shape[0]
    chunks = k_ref.shape[0] // B_K_ROWS
    assert chunks % 2 == 0 and chunks >= 4
    heads = [_group(hh) for hh in range(2)]

    def rows_of(c):
        return pl.ds(pl.multiple_of(c * B_K_ROWS, B_K_ROWS), B_K_ROWS)

    def scores(c, s_ref):
        for hh, sl in enumerate(heads):
            s_ref[hh] = _dot_nt(q_ref[:, sl], k_ref[rows_of(c), sl])

    def softmax(s_ref, p_ref, maxes):
        out = []
        for hh, m in enumerate(maxes):
            s = s_ref[hh]
            m_new = jnp.maximum(m, jnp.max(s, axis=-1, keepdims=True))
            p_ref[hh] = jnp.exp2(s - m_new).astype(_BF16)
            out.append((m_new, jnp.exp2(m - m_new)))
        return [o[0] for o in out], [o[1] for o in out]

    def values(c, p_ref, alphas, accs):
        return [alpha * acc + _dot(p_ref[hh], v_ref[rows_of(c), sl])
                for hh, (sl, alpha, acc) in enumerate(zip(heads, alphas, accs))]

    maxes = [jnp.full((rows, 1), NEG, _F32)] * 2
    accs = [jnp.zeros((rows, HEAD_PAD), _F32)] * 2
    scores(0, s_even)
    scores(1, s_odd)
    maxes, alphas = softmax(s_even, p_even, maxes)

    def pair(j, carry):
        maxes, alphas, accs = carry
        c = 2 * j + 1
        scores(c + 1, s_even)
        maxes, alphas_odd = softmax(s_odd, p_odd, maxes)
        accs = values(c - 1, p_even, alphas, accs)
        scores(c + 2, s_odd)
        maxes, alphas = softmax(s_even, p_even, maxes)
        accs = values(c, p_odd, alphas_odd, accs)
        return maxes, alphas, accs

    maxes, alphas, accs = lax.fori_loop(0, chunks // 2 - 1, pair, (maxes, alphas, accs), unroll=True)
    maxes, alphas_odd = softmax(s_odd, p_odd, maxes)
    accs = values(chunks - 2, p_even, alphas, accs)
    accs = values(chunks - 1, p_odd, alphas_odd, accs)

    lane = lax.broadcasted_iota(jnp.int32, (1, LANES), 1)
    normed = [acc / acc[:, V_HEAD_DIM:V_HEAD_DIM + 1] for acc in accs]
    second = pltpu.roll(normed[1], V_HEAD_DIM, axis=1)
    o_ref[...] = jnp.where(lane < V_HEAD_DIM, normed[0], second).astype(_BF16)


def _mixer_b_call(qb, kb, vb):
    b, s, _ = qb.shape
    pairs = N_HEADS_B // 2
    kv_spec = pl.BlockSpec((None, s, 2 * HEAD_PAD), lambda bi, g, i: (bi, 0, g))
    return pl.pallas_call(
        _mixer_b_kernel,
        grid=(b, pairs, s // B_Q_ROWS),
        in_specs=[pl.BlockSpec((None, B_Q_ROWS, 2 * HEAD_PAD), lambda bi, g, i: (bi, i, g)), kv_spec, kv_spec],
        out_specs=pl.BlockSpec((None, B_Q_ROWS, 2 * V_HEAD_DIM), lambda bi, g, i: (bi, i, g)),
        out_shape=jax.ShapeDtypeStruct((b, s, WIDTH_B), _BF16),
        scratch_shapes=[pltpu.VMEM((2, B_Q_ROWS, B_K_ROWS), _F32)] * 2
                       + [pltpu.VMEM((2, B_Q_ROWS, B_K_ROWS), _BF16)] * 2,
        name="mixer_b",
        compiler_params=_params(3),
    )(qb, kb, vb)


def _merge_kernel(x_ref, *refs):
    n_dil = len(DILATIONS)
    o_refs, lse_refs = refs[:n_dil], refs[n_dil:2 * n_dil]
    ob_ref, g_a_ref, g_b_ref, w_o_ref, x1_ref, o_stage, lse_stage = refs[2 * n_dil:]
    rows = x_ref.shape[0]
    for j, dil in enumerate(DILATIONS):
        for g in range(GROUPS_A):
            for r in range(dil):
                rows_r = pl.ds(r, rows // dil, stride=dil) if dil > 1 else pl.ds(0, rows)
                o_stage[j * GROUPS_A + g, rows_r, :] = o_refs[j][r, :, _group(g)].astype(_F32)
                lse_stage[j * GROUPS_A + g, rows_r, :] = lse_refs[j][r, :, _group(g)]
    merged = []
    for g in range(GROUPS_A):
        lses = [lse_stage[j * GROUPS_A + g] for j in range(n_dil)]
        top = lses[0]
        for l in lses[1:]:
            top = jnp.maximum(top, l)
        wts = [jnp.exp2(l - top) for l in lses]
        num = sum(w * o_stage[j * GROUPS_A + g] for j, w in enumerate(wts))
        merged.append(num / sum(wts))
    n_a = _rms(jnp.concatenate(merged, axis=-1), g_a_ref[...]).astype(_BF16)
    n_b = _rms(ob_ref[...].astype(_F32), g_b_ref[...]).astype(_BF16)
    x1_ref[...] = x_ref[...] + _dot(n_a, w_o_ref[:WIDTH_A, :]) + _dot(n_b, w_o_ref[WIDTH_A:, :])


def _merge_call(x, o_pats, lse_pats, o_b, g_a, g_b, w_o):
    b, s, _ = x.shape
    rows = PROJ_ROWS
    tok = lambda width: pl.BlockSpec((None, rows, width), lambda bi, i: (bi, i, 0))
    pat_specs = [_residue_spec(dil, rows) for dil in DILATIONS]
    stage = pltpu.VMEM((len(DILATIONS) * GROUPS_A, rows, LANES), _F32)
    return pl.pallas_call(
        _merge_kernel,
        grid=(b, s // rows),
        in_specs=[tok(D_MODEL)] + pat_specs * 2 + [tok(WIDTH_B)]
                 + [_const_spec(g_a.shape), _const_spec(g_b.shape), _const_spec(w_o.shape)],
        out_specs=tok(D_MODEL),
        out_shape=jax.ShapeDtypeStruct((b, s, D_MODEL), _F32),
        scratch_shapes=[stage, stage],
        name="merge",
        compiler_params=_params(2),
    )(x, *o_pats, *lse_pats, o_b, g_a, g_b, w_o)


def _ffn_kernel(x_ref, prev_ref, next_ref, g_ffn_ref, w_up_ref, w_conv_ref, b_conv_ref, w_down_ref, g_final_ref,
                y_ref, act_ref):
    i, last = pl.program_id(1), pl.num_programs(1) - 1
    rows = x_ref.shape[0]
    g = g_ffn_ref[...]
    x = x_ref[...]
    h_prev = jnp.where(i > 0, _rms(prev_ref[...], g), 0.0)
    h_next = jnp.where(i < last, _rms(next_ref[...], g), 0.0)
    hn = jnp.concatenate([h_prev, _rms(x, g), h_next], axis=0).astype(_BF16)
    ext = rows + 2 * SUBLANES

    def conv(lo):
        u = _dot(hn, w_up_ref[:, lo:lo + FFN_COLS])
        w = w_conv_ref[:, lo:lo + FFN_COLS]
        before = pltpu.roll(u, 1, axis=0)[SUBLANES:SUBLANES + rows]
        after = pltpu.roll(u, ext - 1, axis=0)[SUBLANES:SUBLANES + rows]
        return (before * w[0:1] + u[SUBLANES:SUBLANES + rows] * w[1:2] + after * w[2:3]
                + b_conv_ref[:, lo:lo + FFN_COLS])

    def activations(chunks):
        for c in chunks:
            gate, val = conv(c * FFN_COLS), conv(D_FF + c * FFN_COLS)
            act_ref[:, c * FFN_COLS:(c + 1) * FFN_COLS] = (gate * jax.nn.sigmoid(gate) * val).astype(_BF16)

    split = FFN_SPLIT * FFN_COLS
    activations(range(FFN_SPLIT))
    acc = _dot(act_ref[:, :split], w_down_ref[:split, :])
    activations(range(FFN_SPLIT, D_FF // FFN_COLS))
    acc = acc + _dot(act_ref[:, split:], w_down_ref[split:, :])
    y_ref[...] = _rms(x + acc, g_final_ref[...])


def _ffn_call(x1, g_ffn, w_up, w_conv, b_conv, w_down, g_final):
    b, s, _ = x1.shape
    rows = FFN_ROWS
    resident = lambda w: pl.BlockSpec(w.shape, lambda *_: (0,) * w.ndim, pipeline_mode=pl.Buffered(1))
    blocks = rows // SUBLANES
    last_block = s // SUBLANES - 1
    tok = pl.BlockSpec((None, rows, D_MODEL), lambda bi, i: (bi, i, 0))
    prev = pl.BlockSpec((None, SUBLANES, D_MODEL), lambda bi, i: (bi, jnp.maximum(i * blocks - 1, 0), 0))
    nxt = pl.BlockSpec((None, SUBLANES, D_MODEL), lambda bi, i: (bi, jnp.minimum((i + 1) * blocks, last_block), 0))
    return pl.pallas_call(
        _ffn_kernel,
        grid=(b, s // rows),
        in_specs=[tok, prev, nxt, _const_spec(g_ffn.shape), resident(w_up), _const_spec(w_conv.shape),
                  _const_spec(b_conv.shape), resident(w_down), _const_spec(g_final.shape)],
        out_specs=tok,
        out_shape=jax.ShapeDtypeStruct((b, s, D_MODEL), _F32),
        scratch_shapes=[pltpu.VMEM((rows, D_FF), _BF16)],
        name="ffn",
        compiler_params=_params(2),
    )(x1, x1, x1, g_ffn, w_up, w_conv, b_conv, w_down, g_final)


def _pad_heads(w, n_heads, lo, width):
    k = w.shape[0]
    w3 = w.reshape(k, n_heads, -1)[..., lo:lo + width]
    return jnp.pad(w3, ((0, 0), (0, 0), (0, HEAD_PAD - width))).reshape(k, n_heads * HEAD_PAD)


def _prepare_weights(w_in, w_uq, w_ukv):
    wq, wk, wv = (w_in[:, j * WIDTH_A:(j + 1) * WIDTH_A] for j in range(3))
    lo = 3 * WIDTH_A
    w_cq = w_in[:, lo:lo + Q_LORA_RANK]
    w_ckv = w_in[:, lo + Q_LORA_RANK:lo + Q_LORA_RANK + KV_LORA_RANK]
    w_kr = w_in[:, lo + Q_LORA_RANK + KV_LORA_RANK:]
    place = lambda w: jnp.pad(w, ((0, 0), (QK_NOPE_DIM, HEAD_PAD - QK_NOPE_DIM - QK_ROPE_DIM)))
    w_ext = jnp.concatenate([wq, wk, wv, w_cq, w_ckv, place(w_kr)], axis=1).astype(_BF16)
    assert w_ext.shape[1] == _EXT_WIDTH
    w_uq_ext = _pad_heads(w_uq, N_HEADS_B, 0, QK_NOPE_DIM + QK_ROPE_DIM).astype(_BF16)

    w_ukv_ext = jnp.concatenate([_pad_heads(w_ukv, N_HEADS_B, 0, QK_NOPE_DIM),
                                 _pad_heads(w_ukv, N_HEADS_B, QK_NOPE_DIM, V_HEAD_DIM)], axis=1).astype(_BF16)
    return w_ext, w_uq_ext, w_ukv_ext


def _rope_tables(s):
    pos = jnp.arange(s, dtype=_F32)[:, None]

    def cos_sin(dim):
        half = dim // 2
        inv_freq = ROPE_THETA ** (-jnp.arange(half, dtype=_F32) * (2.0 / dim))
        ang = pos * inv_freq[None, :]
        return jnp.tile(jnp.cos(ang), (1, 2)), jnp.concatenate([-jnp.sin(ang), jnp.sin(ang)], axis=1)

    cos_a, sin_a = (jnp.tile(t, (1, LANES // HEAD_DIM_A)) for t in cos_sin(HEAD_DIM_A))
    scale_a = HEAD_DIM_A ** -0.5 * LOG2_E
    tab_a = jnp.concatenate([cos_a * scale_a, sin_a * scale_a, cos_a, sin_a], axis=1)

    cos_b, sin_b = cos_sin(QK_ROPE_DIM)
    tail = HEAD_PAD - QK_NOPE_DIM - QK_ROPE_DIM
    lay = lambda nope, t: jnp.concatenate([jnp.full((s, QK_NOPE_DIM), nope, _F32), t, jnp.zeros((s, tail), _F32)], 1)
    scale_b = (QK_NOPE_DIM + QK_ROPE_DIM) ** -0.5 * LOG2_E
    tab_b = jnp.concatenate([lay(1.0, cos_b) * scale_b, lay(0.0, sin_b) * scale_b, lay(0.0, cos_b), lay(0.0, sin_b)],
                            axis=1)
    return tab_a, tab_b


def _layer(x, tabs, g_attn, w_in, g_q_lora, w_uq, g_kv_lora, w_ukv, g_out_a, g_out_b, w_o):
    row = lambda g: g.reshape(1, -1)
    w_ext, w_uq_ext, w_ukv_ext = _prepare_weights(w_in, w_uq, w_ukv)
    a_ops, (qb, kb, vb) = _proj_call(x, *tabs, row(g_attn), w_ext, row(g_q_lora), w_uq_ext, row(g_kv_lora),
                                     w_ukv_ext)
    pats = [_mixer_a_call(*ops) for ops in a_ops]
    o_b = _mixer_b_call(qb, kb, vb)
    return _merge_call(x, [p[0] for p in pats], [p[1] for p in pats], o_b, row(g_out_a), row(g_out_b),
                       w_o.astype(_BF16))


def _trunk(x, g_attn, w_in, g_q_lora, w_uq, g_kv_lora, w_ukv, g_out_a, g_out_b, w_o, g_ffn, w_up, w_conv, b_conv,
           w_down, g_final):
    depth = w_in.shape[0]
    assert depth == 1, "the ffn kernel applies the final norm, so it closes the trunk"
    assert all(window // 2 // dil == A_RADIUS for window, dil in DILATED_CONFIGS)
    tabs = _rope_tables(x.shape[1])
    x1 = _layer(x, tabs, g_attn[0], w_in[0], g_q_lora[0], w_uq[0], g_kv_lora[0], w_ukv[0], g_out_a[0], g_out_b[0],
                w_o[0])
    return _ffn_call(x1, g_ffn[0].reshape(1, -1), w_up[0].astype(_BF16), w_conv[0], b_conv[0].reshape(1, -1),
                     w_down[0].astype(_BF16), g_final.reshape(1, -1))


def kernel(x_prompt, x_sample, g_attn, w_in, g_q_lora, w_uq, g_kv_lora, w_ukv, g_out_a, g_out_b, w_o, g_ffn, w_up,
           w_conv, b_conv, w_down, g_final):
    weights = (g_attn, w_in, g_q_lora, w_uq, g_kv_lora, w_ukv, g_out_a, g_out_b, w_o, g_ffn, w_up, w_conv, b_conv,
               w_down, g_final)
    return (_trunk(x_prompt, *weights), _trunk(x_sample, *weights))
```

```python
import jax
import jax.numpy as jnp
from jax import lax
from jax.experimental import pallas as pl
from jax.experimental.pallas import tpu as pltpu

D_MODEL = 1024
N_HEADS_A = 8
HEAD_DIM_A = 64
DILATED_CONFIGS = ((128, 1), (512, 4), (2048, 16))
N_HEADS_B = 8
Q_LORA_RANK = 256
KV_LORA_RANK = 128
QK_NOPE_DIM = 64
QK_ROPE_DIM = 32
V_HEAD_DIM = 64
WIDTH_A = N_HEADS_A * HEAD_DIM_A
WIDTH_B = N_HEADS_B * V_HEAD_DIM
D_FF = 2816
ROPE_THETA = 10000.0
EPS = 1e-6
NEG = -1e30
LOG2_E = 1.4426950408889634

LANES = 128
SUBLANES = 8
HEAD_PAD = LANES
WIDTH_B_PAD = N_HEADS_B * HEAD_PAD
GROUPS_A = WIDTH_A // LANES
VMEM_LIMIT_BYTES = 56 * 1024 * 1024

DILATIONS = tuple(dil for _, dil in DILATED_CONFIGS)
PROJ_ROWS = 512
FFN_ROWS = 1024
A_ROWS = 128
A_RADIUS = 64
A_WINDOW = A_ROWS + 2 * A_RADIUS
B_Q_ROWS = 512
B_K_ROWS = 512
FFN_COLS = 256
FFN_SPLIT = 6

_OFF_Q, _OFF_K, _OFF_V = 0, WIDTH_A, 2 * WIDTH_A
_OFF_CQ = 3 * WIDTH_A
_OFF_CKV = _OFF_CQ + Q_LORA_RANK
_EXT_WIDTH = _OFF_CKV + KV_LORA_RANK + HEAD_PAD

_BF16 = jnp.bfloat16
_F32 = jnp.float32


def _rms(xf, g):
    return xf * lax.rsqrt(jnp.mean(xf * xf, axis=-1, keepdims=True) + EPS) * g


def _dot(a, b):
    return jnp.dot(a, b, preferred_element_type=_F32)


def _dot_nt(a, b):
    return lax.dot_general(a, b, (((1,), (1,)), ((), ())), preferred_element_type=_F32)


def _rope(t, cos, sin_signed, half, first_half):
    partner = jnp.where(first_half, pltpu.roll(t, LANES - half, axis=1), pltpu.roll(t, half, axis=1))
    return t * cos + partner * sin_signed


def _group(g):
    return slice(g * LANES, (g + 1) * LANES)


def _const_spec(shape):
    return pl.BlockSpec(shape, lambda *_: (0,) * len(shape))


def _params(n_axes):
    return pltpu.CompilerParams(dimension_semantics=("parallel",) * n_axes, vmem_limit_bytes=VMEM_LIMIT_BYTES)


def _proj_kernel(x_ref, tab_a_ref, tab_b_ref, g_attn_ref, w_ext_ref, g_q_ref, w_uq_ref, g_kv_ref, w_ukv_ref, *refs):
    n_dil = len(DILATIONS)
    a_refs = [refs[3 * j:3 * j + 3] for j in range(n_dil)]
    qb_ref, kb_ref, vb_ref = refs[3 * n_dil:3 * n_dil + 3]
    stage_refs = refs[3 * n_dil + 3:3 * n_dil + 6]
    levels = n_dil - 2
    level_refs = [refs[3 * n_dil + 6 + j * levels:3 * n_dil + 6 + (j + 1) * levels] for j in range(3)]
    rows = x_ref.shape[0]
    hn = _rms(x_ref[...], g_attn_ref[...]).astype(_BF16)

    def proj(lo, width):
        return _dot(hn, w_ext_ref[:, lo:lo + width])

    def tab(ref, j):
        return ref[:, _group(j)]

    lane = lax.broadcasted_iota(jnp.int32, (1, LANES), 1)
    half_a, half_b = HEAD_DIM_A // 2, QK_ROPE_DIM // 2
    first_a, first_b = (lane % HEAD_DIM_A) < half_a, (lane % QK_ROPE_DIM) < half_b

    cqn = _rms(proj(_OFF_CQ, Q_LORA_RANK), g_q_ref[...]).astype(_BF16)
    qb = _dot(cqn, w_uq_ref[...])
    for h in range(N_HEADS_B):
        qb_ref[:, _group(h)] = _rope(qb[:, _group(h)], tab(tab_b_ref, 0), tab(tab_b_ref, 1), half_b,
                                     first_b).astype(_BF16)

    rest = proj(_OFF_CKV, KV_LORA_RANK + HEAD_PAD)
    ckvn = _rms(rest[:, :KV_LORA_RANK], g_kv_ref[...]).astype(_BF16)
    kv = _dot(ckvn, w_ukv_ref[...])
    kb_rot = _rope(rest[:, KV_LORA_RANK:], tab(tab_b_ref, 2), tab(tab_b_ref, 3), half_b, first_b)
    one_col = (lane == V_HEAD_DIM).astype(_F32)
    for h in range(N_HEADS_B):
        kb_ref[:, _group(h)] = (kv[:, _group(h)] + kb_rot).astype(_BF16)
        vb_ref[:, _group(h)] = (kv[:, WIDTH_B_PAD + h * HEAD_PAD:WIDTH_B_PAD + (h + 1) * HEAD_PAD]
                                + one_col).astype(_BF16)

    for j, off in enumerate((_OFF_Q, _OFF_K, _OFF_V)):
        stage_ref = stage_refs[j]
        main = proj(off, WIDTH_A)
        for g in range(GROUPS_A):
            if off == _OFF_V:
                stage_ref[g] = main[:, _group(g)]
            else:
                stage_ref[g] = _rope(main[:, _group(g)], tab(tab_a_ref, 2 * j), tab(tab_a_ref, 2 * j + 1), half_a,
                                     first_a)
        src_ref, src_dil = stage_ref, 1
        for level, (outs, dil) in enumerate(zip(a_refs, DILATIONS)):
            step = dil // src_dil
            dst_ref = level_refs[j][level - 1] if 0 < level < n_dil - 1 else None
            for g in range(GROUPS_A):
                for r in range(dil):
                    base = (r % src_dil) * (rows // src_dil) + r // src_dil
                    rows_r = pl.ds(base, rows // dil, stride=step) if step > 1 else pl.ds(base, rows // dil)
                    piece = src_ref[g, rows_r, :]
                    outs[j][r, :, _group(g)] = piece.astype(_BF16)
                    if dst_ref is not None:
                        dst_ref[g, pl.ds(r * (rows // dil), rows // dil), :] = piece
            if dst_ref is not None:
                src_ref, src_dil = dst_ref, dil


def _residue_spec(dil, rows):
    return pl.BlockSpec((None, dil, rows // dil, WIDTH_A), lambda bi, i: (bi, 0, i, 0))


def _proj_call(x, tab_a, tab_b, g_attn, w_ext, g_q, w_uq, g_kv, w_ukv):
    b, s, _ = x.shape
    rows = PROJ_ROWS
    tok = lambda width: pl.BlockSpec((None, rows, width), lambda bi, i: (bi, i, 0))
    tabs = lambda width: pl.BlockSpec((rows, width), lambda bi, i: (i, 0))
    out_b = jax.ShapeDtypeStruct((b, s, WIDTH_B_PAD), _BF16)
    a_specs = [_residue_spec(dil, rows) for dil in DILATIONS for _ in range(3)]
    a_shapes = [jax.ShapeDtypeStruct((b, dil, s // dil, WIDTH_A), _BF16) for dil in DILATIONS for _ in range(3)]
    outs = pl.pallas_call(
        _proj_kernel,
        grid=(b, s // rows),
        in_specs=[tok(D_MODEL), tabs(4 * LANES), tabs(4 * LANES),
                  _const_spec(g_attn.shape), _const_spec(w_ext.shape), _const_spec(g_q.shape),
                  _const_spec(w_uq.shape), _const_spec(g_kv.shape), _const_spec(w_ukv.shape)],
        out_specs=a_specs + [tok(WIDTH_B_PAD)] * 3,
        out_shape=a_shapes + [out_b] * 3,
        scratch_shapes=[pltpu.VMEM((GROUPS_A, rows, LANES), _F32)] * (3 + 3 * (len(DILATIONS) - 2)),
        name="proj",
        compiler_params=_params(2),
    )(x, tab_a, tab_b, g_attn, w_ext, g_q, w_uq, g_kv, w_ukv)
    n = 3 * len(DILATIONS)
    return [outs[3 * j:3 * j + 3] for j in range(len(DILATIONS))], outs[n:]


def _mixer_a_kernel(q_ref, k_ref, v_ref, o_ref, lse_ref, bias_ref):
    residues, length, _ = q_ref.shape
    tiles = length // A_ROWS
    lane = lax.broadcasted_iota(jnp.int32, (1, LANES), 1)
    first_head = lane < HEAD_DIM_A
    col_minus_row = (lax.broadcasted_iota(jnp.int32, (A_ROWS, A_WINDOW), 1)
                     - lax.broadcasted_iota(jnp.int32, (A_ROWS, A_WINDOW), 0))
    for case in range(3):
        bias_ref[case] = jnp.where(jnp.abs(col_minus_row - case * A_RADIUS) <= A_RADIUS, 0.0, NEG)

    def tile(t, carry):
        r = t // tiles
        a0 = pl.multiple_of((t % tiles) * A_ROWS, A_ROWS)
        ws = pl.multiple_of(jnp.clip(a0 - A_RADIUS, 0, length - A_WINDOW), A_RADIUS)
        bias = bias_ref[(a0 - ws) // A_RADIUS]
        q = q_ref[r, pl.ds(a0, A_ROWS), :]
        kw = k_ref[r, pl.ds(ws, A_WINDOW), :]
        vw = v_ref[r, pl.ds(ws, A_WINDOW), :]
        nums, maxes, dens = [], [], []
        for head_sel in (first_head, jnp.logical_not(first_head)):
            s = _dot_nt(jnp.where(head_sel, q, jnp.zeros_like(q)), kw) + bias
            m = jnp.max(s, axis=-1, keepdims=True)
            p = jnp.exp2(s - m)
            nums.append(_dot(p.astype(_BF16), vw))
            maxes.append(m)
            dens.append(jnp.sum(p, axis=-1, keepdims=True))
        num, m, den = (jnp.where(first_head, a, b) for a, b in (nums, maxes, dens))
        o_ref[r, pl.ds(a0, A_ROWS), :] = (num / den).astype(_BF16)
        lse_ref[r, pl.ds(a0, A_ROWS), :] = m + jnp.log2(den)
        return carry

    lax.fori_loop(0, residues * tiles, tile, 0, unroll=True)


def _mixer_a_call(qa, ka, va):
    b, dil, length, _ = qa.shape
    assert length % A_ROWS == 0 and length >= A_WINDOW
    spec = pl.BlockSpec((None, dil, length, LANES), lambda bi, g: (bi, 0, 0, g))
    return pl.pallas_call(
        _mixer_a_kernel,
        grid=(b, GROUPS_A),
        in_specs=[spec] * 3,
        out_specs=[spec] * 2,
        out_shape=[jax.ShapeDtypeStruct(qa.shape, _BF16), jax.ShapeDtypeStruct(qa.shape, _F32)],
        scratch_shapes=[pltpu.VMEM((3, A_ROWS, A_WINDOW), _F32)],
        name=f"mixer_a_dil{dil}",
        compiler_params=_params(2),
    )(qa, ka, va)


def _mixer_b_kernel(q_ref, k_ref, v_ref, o_ref, s_even, s_odd, p_even, p_odd):
    rows = q_ref.shape[0]
    chunks = k_ref.shape[0] // B_K_ROWS
    assert chunks % 2 == 0 and chunks >= 4
    heads = [_group(hh) for hh in range(2)]

    def rows_of(c):
        return pl.ds(pl.multiple_of(c * B_K_ROWS, B_K_ROWS), B_K_ROWS)

    def scores(c, s_ref):
        for hh, sl in enumerate(heads):
            s_ref[hh] = _dot_nt(q_ref[:, sl], k_ref[rows_of(c), sl])

    def softmax(s_ref, p_ref, maxes):
        out = []
        for hh, m in enumerate(maxes):
            s = s_ref[hh]
            m_new = jnp.maximum(m, jnp.max(s, axis=-1, keepdims=True))
            p_ref[hh] = jnp.exp2(s - m_new).astype(_BF16)
            out.append((m_new, jnp.exp2(m - m_new)))
        return [o[0] for o in out], [o[1] for o in out]

    def values(c, p_ref, alphas, accs):
        return [alpha * acc + _dot(p_ref[hh], v_ref[rows_of(c), sl])
                for hh, (sl, alpha, acc) in enumerate(zip(heads, alphas, accs))]

    maxes = [jnp.full((rows, 1), NEG, _F32)] * 2
    accs = [jnp.zeros((rows, HEAD_PAD), _F32)] * 2
    scores(0, s_even)
    scores(1, s_odd)
    maxes, alphas = softmax(s_even, p_even, maxes)

    def pair(j, carry):
        maxes, alphas, accs = carry
        c = 2 * j + 1
        scores(c + 1, s_even)
        maxes, alphas_odd = softmax(s_odd, p_odd, maxes)
        accs = values(c - 1, p_even, alphas, accs)
        scores(c + 2, s_odd)
        maxes, alphas = softmax(s_even, p_even, maxes)
        accs = values(c, p_odd, alphas_odd, accs)
        return maxes, alphas, accs

    maxes, alphas, accs = lax.fori_loop(0, chunks // 2 - 1, pair, (maxes, alphas, accs), unroll=True)
    maxes, alphas_odd = softmax(s_odd, p_odd, maxes)
    accs = values(chunks - 2, p_even, alphas, accs)
    accs = values(chunks - 1, p_odd, alphas_odd, accs)

    lane = lax.broadcasted_iota(jnp.int32, (1, LANES), 1)
    normed = [acc / acc[:, V_HEAD_DIM:V_HEAD_DIM + 1] for acc in accs]
    second = pltpu.roll(normed[1], V_HEAD_DIM, axis=1)
    o_ref[...] = jnp.where(lane < V_HEAD_DIM, normed[0], second).astype(_BF16)


def _mixer_b_call(qb, kb, vb):
    b, s, _ = qb.shape
    pairs = N_HEADS_B // 2
    kv_spec = pl.BlockSpec((None, s, 2 * HEAD_PAD), lambda bi, g, i: (bi, 0, g))
    return pl.pallas_call(
        _mixer_b_kernel,
        grid=(b, pairs, s // B_Q_ROWS),
        in_specs=[pl.BlockSpec((None, B_Q_ROWS, 2 * HEAD_PAD), lambda bi, g, i: (bi, i, g)), kv_spec, kv_spec],
        out_specs=pl.BlockSpec((None, B_Q_ROWS, 2 * V_HEAD_DIM), lambda bi, g, i: (bi, i, g)),
        out_shape=jax.ShapeDtypeStruct((b, s, WIDTH_B), _BF16),
        scratch_shapes=[pltpu.VMEM((2, B_Q_ROWS, B_K_ROWS), _F32)] * 2
                       + [pltpu.VMEM((2, B_Q_ROWS, B_K_ROWS), _BF16)] * 2,
        name="mixer_b",
        compiler_params=_params(3),
    )(qb, kb, vb)


def _merge_kernel(x_ref, *refs):
    n_dil = len(DILATIONS)
    o_refs, lse_refs = refs[:n_dil], refs[n_dil:2 * n_dil]
    ob_ref, g_a_ref, g_b_ref, w_o_ref, x1_ref, o_stage, lse_stage = refs[2 * n_dil:]
    rows = x_ref.shape[0]
    for j, dil in enumerate(DILATIONS):
        for g in range(GROUPS_A):
            for r in range(dil):
                rows_r = pl.ds(r, rows // dil, stride=dil) if dil > 1 else pl.ds(0, rows)
                o_stage[j * GROUPS_A + g, rows_r, :] = o_refs[j][r, :, _group(g)].astype(_F32)
                lse_stage[j * GROUPS_A + g, rows_r, :] = lse_refs[j][r, :, _group(g)]
    merged = []
    for g in range(GROUPS_A):
        lses = [lse_stage[j * GROUPS_A + g] for j in range(n_dil)]
        top = lses[0]
        for l in lses[1:]:
            top = jnp.maximum(top, l)
        wts = [jnp.exp2(l - top) for l in lses]
        num = sum(w * o_stage[j * GROUPS_A + g] for j, w in enumerate(wts))
        merged.append(num / sum(wts))
    n_a = _rms(jnp.concatenate(merged, axis=-1), g_a_ref[...]).astype(_BF16)
    n_b = _rms(ob_ref[...].astype(_F32), g_b_ref[...]).astype(_BF16)
    x1_ref[...] = x_ref[...] + _dot(n_a, w_o_ref[:WIDTH_A, :]) + _dot(n_b, w_o_ref[WIDTH_A:, :])


def _merge_call(x, o_pats, lse_pats, o_b, g_a, g_b, w_o):
    b, s, _ = x.shape
    rows = PROJ_ROWS
    tok = lambda width: pl.BlockSpec((None, rows, width), lambda bi, i: (bi, i, 0))
    pat_specs = [_residue_spec(dil, rows) for dil in DILATIONS]
    stage = pltpu.VMEM((len(DILATIONS) * GROUPS_A, rows, LANES), _F32)
    return pl.pallas_call(
        _merge_kernel,
        grid=(b, s // rows),
        in_specs=[tok(D_MODEL)] + pat_specs * 2 + [tok(WIDTH_B)]
                 + [_const_spec(g_a.shape), _const_spec(g_b.shape), _const_spec(w_o.shape)],
        out_specs=tok(D_MODEL),
        out_shape=jax.ShapeDtypeStruct((b, s, D_MODEL), _F32),
        scratch_shapes=[stage, stage],
        name="merge",
        compiler_params=_params(2),
    )(x, *o_pats, *lse_pats, o_b, g_a, g_b, w_o)


def _ffn_kernel(x_ref, prev_ref, next_ref, g_ffn_ref, w_up_ref, w_conv_ref, b_conv_ref, w_down_ref, g_final_ref,
                y_ref, act_ref):
    i, last = pl.program_id(1), pl.num_programs(1) - 1
    rows = x_ref.shape[0]
    g = g_ffn_ref[...]
    x = x_ref[...]
    h_prev = jnp.where(i > 0, _rms(prev_ref[...], g), 0.0)
    h_next = jnp.where(i < last, _rms(next_ref[...], g), 0.0)
    hn = jnp.concatenate([h_prev, _rms(x, g), h_next], axis=0).astype(_BF16)
    ext = rows + 2 * SUBLANES

    def conv(lo):
        u = _dot(hn, w_up_ref[:, lo:lo + FFN_COLS])
        w = w_conv_ref[:, lo:lo + FFN_COLS]
        before = pltpu.roll(u, 1, axis=0)[SUBLANES:SUBLANES + rows]
        after = pltpu.roll(u, ext - 1, axis=0)[SUBLANES:SUBLANES + rows]
        return (before * w[0:1] + u[SUBLANES:SUBLANES + rows] * w[1:2] + after * w[2:3]
                + b_conv_ref[:, lo:lo + FFN_COLS])

    def activations(chunks):
        for c in chunks:
            gate, val = conv(c * FFN_COLS), conv(D_FF + c * FFN_COLS)
            act_ref[:, c * FFN_COLS:(c + 1) * FFN_COLS] = (gate * jax.nn.sigmoid(gate) * val).astype(_BF16)

    split = FFN_SPLIT * FFN_COLS
    activations(range(FFN_SPLIT))
    acc = _dot(act_ref[:, :split], w_down_ref[:split, :])
    activations(range(FFN_SPLIT, D_FF // FFN_COLS))
    acc = acc + _dot(act_ref[:, split:], w_down_ref[split:, :])
    y_ref[...] = _rms(x + acc, g_final_ref[...])


def _ffn_call(x1, g_ffn, w_up, w_conv, b_conv, w_down, g_final):
    b, s, _ = x1.shape
    rows = FFN_ROWS
    resident = lambda w: pl.BlockSpec(w.shape, lambda *_: (0,) * w.ndim, pipeline_mode=pl.Buffered(1))
    blocks = rows // SUBLANES
    last_block = s // SUBLANES - 1
    tok = pl.BlockSpec((None, rows, D_MODEL), lambda bi, i: (bi, i, 0))
    prev = pl.BlockSpec((None, SUBLANES, D_MODEL), lambda bi, i: (bi, jnp.maximum(i * blocks - 1, 0), 0))
    nxt = pl.BlockSpec((None, SUBLANES, D_MODEL), lambda bi, i: (bi, jnp.minimum((i + 1) * blocks, last_block), 0))
    return pl.pallas_call(
        _ffn_kernel,
        grid=(b, s // rows),
        in_specs=[tok, prev, nxt, _const_spec(g_ffn.shape), resident(w_up), _const_spec(w_conv.shape),
                  _const_spec(b_conv.shape), resident(w_down), _const_spec(g_final.shape)],
        out_specs=tok,
        out_shape=jax.ShapeDtypeStruct((b, s, D_MODEL), _F32),
        scratch_shapes=[pltpu.VMEM((rows, D_FF), _BF16)],
        name="ffn",
        compiler_params=_params(2),
    )(x1, x1, x1, g_ffn, w_up, w_conv, b_conv, w_down, g_final)


def _pad_heads(w, n_heads, lo, width):
    k = w.shape[0]
    w3 = w.reshape(k, n_heads, -1)[..., lo:lo + width]
    return jnp.pad(w3, ((0, 0), (0, 0), (0, HEAD_PAD - width))).reshape(k, n_heads * HEAD_PAD)


def _prepare_weights(w_in, w_uq, w_ukv):
    wq, wk, wv = (w_in[:, j * WIDTH_A:(j + 1) * WIDTH_A] for j in range(3))
    lo = 3 * WIDTH_A
    w_cq = w_in[:, lo:lo + Q_LORA_RANK]
    w_ckv = w_in[:, lo + Q_LORA_RANK:lo + Q_LORA_RANK + KV_LORA_RANK]
    w_kr = w_in[:, lo + Q_LORA_RANK + KV_LORA_RANK:]
    place = lambda w: jnp.pad(w, ((0, 0), (QK_NOPE_DIM, HEAD_PAD - QK_NOPE_DIM - QK_ROPE_DIM)))
    w_ext = jnp.concatenate([wq, wk, wv, w_cq, w_ckv, place(w_kr)], axis=1).astype(_BF16)
    assert w_ext.shape[1] == _EXT_WIDTH
    w_uq_ext = _pad_heads(w_uq, N_HEADS_B, 0, QK_NOPE_DIM + QK_ROPE_DIM).astype(_BF16)

    w_ukv_ext = jnp.concatenate([_pad_heads(w_ukv, N_HEADS_B, 0, QK_NOPE_DIM),
                                 _pad_heads(w_ukv, N_HEADS_B, QK_NOPE_DIM, V_HEAD_DIM)], axis=1).astype(_BF16)
    return w_ext, w_uq_ext, w_ukv_ext


def _rope_tables(s):
    pos = jnp.arange(s, dtype=_F32)[:, None]

    def cos_sin(dim):
        half = dim // 2
        inv_freq = ROPE_THETA ** (-jnp.arange(half, dtype=_F32) * (2.0 / dim))
        ang = pos * inv_freq[None, :]
        return jnp.tile(jnp.cos(ang), (1, 2)), jnp.concatenate([-jnp.sin(ang), jnp.sin(ang)], axis=1)

    cos_a, sin_a = (jnp.tile(t, (1, LANES // HEAD_DIM_A)) for t in cos_sin(HEAD_DIM_A))
    scale_a = HEAD_DIM_A ** -0.5 * LOG2_E
    tab_a = jnp.concatenate([cos_a * scale_a, sin_a * scale_a, cos_a, sin_a], axis=1)

    cos_b, sin_b = cos_sin(QK_ROPE_DIM)
    tail = HEAD_PAD - QK_NOPE_DIM - QK_ROPE_DIM
    lay = lambda nope, t: jnp.concatenate([jnp.full((s, QK_NOPE_DIM), nope, _F32), t, jnp.zeros((s, tail), _F32)], 1)
    scale_b = (QK_NOPE_DIM + QK_ROPE_DIM) ** -0.5 * LOG2_E
    tab_b = jnp.concatenate([lay(1.0, cos_b) * scale_b, lay(0.0, sin_b) * scale_b, lay(0.0, cos_b), lay(0.0, sin_b)],
                            axis=1)
    return tab_a, tab_b


def _layer(x, tabs, g_attn, w_in, g_q_lora, w_uq, g_kv_lora, w_ukv, g_out_a, g_out_b, w_o):
    row = lambda g: g.reshape(1, -1)
    w_ext, w_uq_ext, w_ukv_ext = _prepare_weights(w_in, w_uq, w_ukv)
    a_ops, (qb, kb, vb) = _proj_call(x, *tabs, row(g_attn), w_ext, row(g_q_lora), w_uq_ext, row(g_kv_lora),
                                     w_ukv_ext)
    pats = [_mixer_a_call(*ops) for ops in a_ops]
    o_b = _mixer_b_call(qb, kb, vb)
    return _merge_call(x, [p[0] for p in pats], [p[1] for p in pats], o_b, row(g_out_a), row(g_out_b),
                       w_o.astype(_BF16))


def _trunk(x, g_attn, w_in, g_q_lora, w_uq, g_kv_lora, w_ukv, g_out_a, g_out_b, w_o, g_ffn, w_up, w_conv, b_conv,
           w_down, g_final):
    depth = w_in.shape[0]
    assert depth == 1, "the ffn kernel applies the final norm, so it closes the trunk"
    assert all(window // 2 // dil == A_RADIUS for window, dil in DILATED_CONFIGS)
    assert DILATIONS[0] == 1 and all(hi % lo == 0 for lo, hi in zip(DILATIONS, DILATIONS[1:]))
    tabs = _rope_tables(x.shape[1])
    x1 = _layer(x, tabs, g_attn[0], w_in[0], g_q_lora[0], w_uq[0], g_kv_lora[0], w_ukv[0], g_out_a[0], g_out_b[0],
                w_o[0])
    return _ffn_call(x1, g_ffn[0].reshape(1, -1), w_up[0].astype(_BF16), w_conv[0], b_conv[0].reshape(1, -1),
                     w_down[0].astype(_BF16), g_final.reshape(1, -1))


def kernel(x_prompt, x_sample, g_attn, w_in, g_q_lora, w_uq, g_kv_lora, w_ukv, g_out_a, g_out_b, w_o, g_ffn, w_up,
           w_conv, b_conv, w_down, g_final):
    weights = (g_attn, w_in, g_q_lora, w_uq, g_kv_lora, w_ukv, g_out_a, g_out_b, w_o, g_ffn, w_up, w_conv, b_conv,
               w_down, g_final)
    return (_trunk(x_prompt, *weights), _trunk(x_sample, *weights))
```
